```python
import math
import jax, jax.numpy as jnp
from jax import lax
import numpy as np

D_MODEL = 2048
BATCH = 4
SEQ = 2048
DEPTH = 1

P_DIM = 256
CONV_CH = 1024
CONV_WIDTH = 31
ATTN_HEADS = 8
ATTN_HEAD_DIM = 64
ATTN_V_DIM = 2 * ATTN_HEAD_DIM
Q_BLOCK = 128
N_BRANCHES = 2
PEER_HEADS = 8
PEER_N_KEYS = 128
PEER_N_EXPERTS = PEER_N_KEYS * PEER_N_KEYS
PEER_KEY_DIM = 256
PEER_HALF = PEER_KEY_DIM // 2
PEER_TOPK = 16
TOKEN_BLOCK = 128
LN_EPS = 1e-5
ALPHA = (2.0 * DEPTH) ** 0.25
BETA = (8.0 * DEPTH) ** -0.25

COL_CONV = 2 * CONV_CH
COL_Q = ATTN_HEADS * 2 * ATTN_HEAD_DIM
COL_K = ATTN_HEADS * 2 * ATTN_HEAD_DIM
COL_V = ATTN_HEADS * ATTN_V_DIM
COL_GATE = N_BRANCHES * D_MODEL
IN_COLS = COL_CONV + COL_Q + COL_K + COL_V + COL_GATE
SPLITS = (COL_CONV, COL_CONV + COL_Q, COL_CONV + COL_Q + COL_K, COL_CONV + COL_Q + COL_K + COL_V)

kernel_name = "hybrid_conv_diffattn_peer_deepnorm"


def lambda_init(layer_idx):
    return 0.8 - 0.6 * math.exp(-0.3 * layer_idx)


def layer_norm(x, g, b):
    xf = x.astype(jnp.float32)
    mu = jnp.mean(xf, axis=-1, keepdims=True)
    var = jnp.mean(jnp.square(xf - mu), axis=-1, keepdims=True)
    return ((xf - mu) * lax.rsqrt(var + LN_EPS)).astype(x.dtype) * g + b


def rms_norm(x, g):
    xf = x.astype(jnp.float32)
    ms = jnp.mean(jnp.square(xf), axis=-1, keepdims=True)
    return (xf * lax.rsqrt(ms + LN_EPS)).astype(x.dtype) * g


def alibi_slopes():
    return jnp.asarray(2.0 ** (-8.0 * np.arange(1, ATTN_HEADS + 1) / ATTN_HEADS), dtype=jnp.float32)


def conformer_conv(a, dw_w, dw_b, ln_g, ln_b, w_pw):
    h = a[..., :CONV_CH] * jax.nn.sigmoid(a[..., CONV_CH:])
    h = lax.conv_general_dilated(
        h, dw_w.astype(h.dtype), window_strides=(1,), padding=[(CONV_WIDTH - 1, 0)],
        dimension_numbers=('NWC', 'WIO', 'NWC'), feature_group_count=CONV_CH) + dw_b
    h = layer_norm(h, ln_g, ln_b)
    h = jax.nn.silu(h)
    return h @ w_pw


def diff_attention(q, k, v, lam, lam_init, subln_g, w_o, slopes):
    B, S, _ = q.shape
    nb = S // Q_BLOCK
    scale = ATTN_HEAD_DIM ** -0.5
    qb = q.reshape(B, nb, Q_BLOCK, ATTN_HEADS, 2, ATTN_HEAD_DIM).transpose(1, 0, 3, 4, 2, 5)
    kh = k.reshape(B, S, ATTN_HEADS, 2, ATTN_HEAD_DIM).transpose(0, 2, 3, 1, 4)
    vh = v.reshape(B, S, ATTN_HEADS, ATTN_V_DIM).transpose(0, 2, 1, 3)
    kpos = jnp.arange(S)

    def block(args):
        qblk, bi = args
        s = jnp.einsum('bhmqd,bhmkd->bhmqk', qblk, kh).astype(jnp.float32) * scale
        qpos = bi * Q_BLOCK + jnp.arange(Q_BLOCK)
        dist = qpos[:, None] - kpos[None, :]
        bias = -slopes[:, None, None] * dist.astype(jnp.float32)[None]
        s = jnp.where((dist >= 0)[None, None, None], s + bias[None, :, None], -jnp.inf)
        pr = jax.nn.softmax(s, axis=-1)
        a = pr[:, :, 0] - lam * pr[:, :, 1]
        return jnp.einsum('bhqk,bhkv->bhqv', a.astype(vh.dtype), vh)

    o = lax.map(block, (qb, jnp.arange(nb)))
    o = o.transpose(1, 0, 3, 2, 4).reshape(B, S, ATTN_HEADS, ATTN_V_DIM)
    o = rms_norm(o, subln_g) * (1.0 - lam_init)
    return o.reshape(B, S, ATTN_HEADS * ATTN_V_DIM) @ w_o


def peer(x, w_q, sub_keys, u_tab, v_tab):
    B, S, D = x.shape
    xt = x.reshape((B * S) // TOKEN_BLOCK, TOKEN_BLOCK, D)

    def block(xb):
        q = (xb @ w_q).reshape(TOKEN_BLOCK, PEER_HEADS, 2, PEER_HALF)
        s = jnp.einsum('thcd,hcnd->thcn', q, sub_keys).astype(jnp.float32)
        vals, idx = lax.top_k(s, PEER_TOPK)
        cand = vals[:, :, 0, :, None] + vals[:, :, 1, None, :]
        cand_ids = idx[:, :, 0, :, None] * PEER_N_KEYS + idx[:, :, 1, None, :]
        best, pos = lax.top_k(cand.reshape(TOKEN_BLOCK, PEER_HEADS, PEER_TOPK * PEER_TOPK), PEER_TOPK)
        ids = jnp.take_along_axis(cand_ids.reshape(TOKEN_BLOCK, PEER_HEADS, PEER_TOPK * PEER_TOPK), pos, axis=-1)
        g = jax.nn.softmax(best, axis=-1)
        u = u_tab[ids]
        v = v_tab[ids]
        h = jax.nn.gelu(jnp.einsum('td,thkd->thk', xb, u))
        w = (g * h.astype(jnp.float32)).astype(xb.dtype)
        return jnp.einsum('thk,thkd->td', w, v)

    return lax.map(block, xt).reshape(B, S, D)


def setup_inputs(seed: int = 0) -> dict:
    key = jax.random.key(seed)
    ks = jax.random.split(key, 32)
    f32 = jnp.float32
    L, D = DEPTH, D_MODEL

    def nrm(k, shape, scale):
        return jax.random.normal(k, shape, f32) * scale

    w_in = jnp.concatenate([
        nrm(ks[2], (L, D, COL_CONV), D ** -0.5),
        nrm(ks[3], (L, D, COL_Q + COL_K), D ** -0.5),
        nrm(ks[4], (L, D, COL_V), BETA * D ** -0.5),
        nrm(ks[5], (L, D, COL_GATE), D ** -0.5),
    ], axis=-1)
    return {
        "x": nrm(ks[0], (BATCH, SEQ, D), 1.0),
        "p": nrm(ks[1], (DEPTH, BATCH, SEQ, P_DIM), 1.0),
        "w_in": w_in,
        "b_gate": nrm(ks[6], (L, COL_GATE), 0.02),
        "conv_dw_w": nrm(ks[7], (L, CONV_WIDTH, 1, CONV_CH), CONV_WIDTH ** -0.5),
        "conv_dw_b": nrm(ks[8], (L, CONV_CH), 0.02),
        "conv_ln_g": 1.0 + nrm(ks[9], (L, CONV_CH), 0.02),
        "conv_ln_b": nrm(ks[10], (L, CONV_CH), 0.02),
        "conv_w_out": nrm(ks[11], (L, CONV_CH, D), BETA * CONV_CH ** -0.5),
        "attn_lambda_q1": nrm(ks[12], (L, ATTN_HEAD_DIM), 0.1),
        "attn_lambda_k1": nrm(ks[13], (L, ATTN_HEAD_DIM), 0.1),
        "attn_lambda_q2": nrm(ks[14], (L, ATTN_HEAD_DIM), 0.1),
        "attn_lambda_k2": nrm(ks[15], (L, ATTN_HEAD_DIM), 0.1),
        "attn_subln_g": 1.0 + nrm(ks[16], (L, ATTN_V_DIM), 0.02),
        "attn_w_o": nrm(ks[17], (L, COL_V, D), BETA * COL_V ** -0.5),
        "w_out": nrm(ks[18], (L, D, D), BETA * D ** -0.5),
        "ln1_g": 1.0 + nrm(ks[19], (L, D), 0.02),
        "ln1_b": nrm(ks[20], (L, D), 0.02),
        "peer_w_q": nrm(ks[21], (L, D, PEER_HEADS * PEER_KEY_DIM), D ** -0.5),
        "peer_sub_keys": nrm(ks[22], (L, PEER_HEADS, 2, PEER_N_KEYS, PEER_HALF), PEER_HALF ** -0.5),
        "peer_u": nrm(ks[23], (L, PEER_N_EXPERTS, D), D ** -0.5),
        "peer_v": nrm(ks[24], (L, PEER_N_EXPERTS, D), BETA),
        "ln2_g": 1.0 + nrm(ks[25], (L, D), 0.02),
        "ln2_b": nrm(ks[26], (L, D), 0.02),
        "ple_w_proj": nrm(ks[27], (L, P_DIM, D), BETA * P_DIM ** -0.5),
        "ple_w_gate": nrm(ks[28], (L, D, D), D ** -0.5),
        "ple_b_gate": nrm(ks[29], (L, D), 0.02),
    }


def reference(x, p, w_in, b_gate, conv_dw_w, conv_dw_b, conv_ln_g, conv_ln_b, conv_w_out,
              attn_lambda_q1, attn_lambda_k1, attn_lambda_q2, attn_lambda_k2, attn_subln_g, attn_w_o,
              w_out, ln1_g, ln1_b, peer_w_q, peer_sub_keys, peer_u, peer_v, ln2_g, ln2_b,
              ple_w_proj, ple_w_gate, ple_b_gate):
    B, S, D = x.shape
    slopes = alibi_slopes()
    for i in range(DEPTH):
        lam_init = lambda_init(i)
        proj = x @ w_in[i]
        a_conv, q, k, v, g_logits = jnp.split(proj, SPLITS, axis=-1)
        y_conv = conformer_conv(a_conv, conv_dw_w[i], conv_dw_b[i], conv_ln_g[i], conv_ln_b[i], conv_w_out[i])
        lam = (jnp.exp(jnp.sum(attn_lambda_q1[i] * attn_lambda_k1[i]).astype(jnp.float32))
               - jnp.exp(jnp.sum(attn_lambda_q2[i] * attn_lambda_k2[i]).astype(jnp.float32)) + lam_init)
        y_attn = diff_attention(q, k, v, lam, lam_init, attn_subln_g[i], attn_w_o[i], slopes)
        gates = jax.nn.sigmoid(g_logits + b_gate[i]).reshape(B, S, N_BRANCHES, D)
        mixed = gates[:, :, 0] * y_conv + gates[:, :, 1] * y_attn
        x = layer_norm(ALPHA * x + mixed @ w_out[i], ln1_g[i], ln1_b[i])
        x = layer_norm(ALPHA * x + peer(x, peer_w_q[i], peer_sub_keys[i], peer_u[i], peer_v[i]), ln2_g[i], ln2_b[i])
        e = p[i] @ ple_w_proj[i]
        x = x + jax.nn.sigmoid(x @ ple_w_gate[i] + ple_b_gate[i]) * e
    return x
```

```python
import functools
import math

import jax
import jax.numpy as jnp
import numpy as np
from jax import lax
from jax.experimental import pallas as pl
from jax.experimental.pallas import tpu as pltpu

D_MODEL = 2048
CONV_CH = 1024
CONV_WIDTH = 31
ATTN_HEADS = 8
ATTN_HEAD_DIM = 64
ATTN_V_DIM = 2 * ATTN_HEAD_DIM
PEER_HEADS = 8
PEER_N_KEYS = 128
PEER_HALF = 128
PEER_TOPK = 16
P_DIM = 256
LN_EPS = 1e-5
DEPTH = 1
ALPHA = (2.0 * DEPTH) ** 0.25
LAM_INIT = 0.8 - 0.6 * math.exp(-0.3 * 0)

COL_CONV = 2 * CONV_CH
COL_QKV = 3 * ATTN_HEADS * ATTN_V_DIM
COL_GATE = 2 * D_MODEL

VMEM_LIMIT_BYTES = 56 * 1024 * 1024
NEG_INF = float("-inf")

CAND_PAIRS = tuple((i, j) for i in range(PEER_TOPK) for j in range(PEER_TOPK)
                   if (i + 1) * (j + 1) <= PEER_TOPK)


def _params(*sem):
    return pltpu.CompilerParams(dimension_semantics=sem, vmem_limit_bytes=VMEM_LIMIT_BYTES)


def _dot(a, b):
    return jnp.dot(a, b, preferred_element_type=jnp.float32)


def _dot_nt(a, b):
    return lax.dot_general(a, b, (((1,), (1,)), ((), ())), preferred_element_type=jnp.float32)


def _dot_tn(a, b):
    return lax.dot_general(a, b, (((0,), (0,)), ((), ())), preferred_element_type=jnp.float32)


def _layer_norm(r, g, b):
    mu = jnp.mean(r, axis=-1, keepdims=True)
    c = r - mu
    var = jnp.mean(c * c, axis=-1, keepdims=True)
    return c * lax.rsqrt(var + LN_EPS) * g + b


def _proj_glu_kernel(x_ref, wa_ref, wb_ref, o_ref):
    x = x_ref[...]
    o_ref[...] = _dot(x, wa_ref[...]) * jax.nn.sigmoid(_dot(x, wb_ref[...]))


def _proj_qkv_kernel(x_ref, w_ref, o_ref):
    o_ref[...] = _dot(x_ref[...], w_ref[...]).astype(o_ref.dtype)


def _proj_gate_kernel(x_ref, w_ref, b_ref, o_ref):
    o_ref[...] = jax.nn.sigmoid(_dot(x_ref[...], w_ref[...]) + b_ref[...])


def _input_projection(xb, w_in_b, b_gate):
    t = xb.shape[0]
    tm, tn = 512, 512
    x_spec = pl.BlockSpec((tm, D_MODEL), lambda i, j: (i, 0))

    def w_spec(col0):
        return pl.BlockSpec((D_MODEL, tn), lambda i, j: (0, j + col0 // tn))

    def o_spec():
        return pl.BlockSpec((tm, tn), lambda i, j: (i, j))

    glu = pl.pallas_call(
        _proj_glu_kernel, name="proj_glu",
        grid=(t // tm, CONV_CH // tn),
        in_specs=[x_spec, w_spec(0), w_spec(CONV_CH)],
        out_specs=o_spec(),
        out_shape=jax.ShapeDtypeStruct((t, CONV_CH), jnp.float32),
        compiler_params=_params("parallel", "parallel"),
    )(xb, w_in_b, w_in_b)
    qkv = pl.pallas_call(
        _proj_qkv_kernel, name="proj_qkv",
        grid=(t // tm, COL_QKV // tn),
        in_specs=[x_spec, w_spec(COL_CONV)],
        out_specs=o_spec(),
        out_shape=jax.ShapeDtypeStruct((t, COL_QKV), jnp.bfloat16),
        compiler_params=_params("parallel", "parallel"),
    )(xb, w_in_b)
    gates = pl.pallas_call(
        _proj_gate_kernel, name="proj_gate",
        grid=(t // tm, COL_GATE // tn),
        in_specs=[x_spec, w_spec(COL_CONV + COL_QKV), pl.BlockSpec((1, tn), lambda i, j: (0, j))],
        out_specs=o_spec(),
        out_shape=jax.ShapeDtypeStruct((t, COL_GATE), jnp.float32),
        compiler_params=_params("parallel", "parallel"),
    )(xb, w_in_b, b_gate.reshape(1, COL_GATE))
    return glu, qkv, gates


CONV_HALO = 32


def _conv_kernel(prev_ref, cur_ref, dww_ref, dwb_ref, lng_ref, lnb_ref, wpw_ref, o_ref, ext_ref, *, ts):
    i = pl.program_id(1)
    halo = prev_ref[0]
    ext_ref[pl.ds(0, CONV_HALO), :] = jnp.where(i > 0, halo, jnp.zeros_like(halo))
    ext_ref[pl.ds(CONV_HALO, ts), :] = cur_ref[0]
    off0 = CONV_HALO - (CONV_WIDTH - 1)
    acc = jnp.zeros((ts, CONV_CH), jnp.float32)
    for j in range(CONV_WIDTH):
        acc = acc + ext_ref[pl.ds(off0 + j, ts), :] * dww_ref[pl.ds(j, 1), :]
    acc = acc + dwb_ref[...]
    y = _layer_norm(acc, lng_ref[...], lnb_ref[...])
    y = y * jax.nn.sigmoid(y)
    o_ref[...] = _dot(y.astype(jnp.bfloat16), wpw_ref[...])


def _conformer_conv(glu, dw_w, dw_b, ln_g, ln_b, w_pw_b, batch, seq):
    ts = 256
    h3 = glu.reshape(batch, seq, CONV_CH)
    nblk = ts // CONV_HALO
    vec = lambda: pl.BlockSpec((1, CONV_CH), lambda b, i: (0, 0))
    return pl.pallas_call(
        functools.partial(_conv_kernel, ts=ts), name="conv_module",
        grid=(batch, seq // ts),
        in_specs=[
            pl.BlockSpec((1, CONV_HALO, CONV_CH), lambda b, i: (b, jnp.maximum(i * nblk - 1, 0), 0)),
            pl.BlockSpec((1, ts, CONV_CH), lambda b, i: (b, i, 0)),
            pl.BlockSpec((CONV_WIDTH, CONV_CH), lambda b, i: (0, 0)),
            vec(), vec(), vec(),
            pl.BlockSpec((CONV_CH, D_MODEL), lambda b, i: (0, 0)),
        ],
        out_specs=pl.BlockSpec((ts, D_MODEL), lambda b, i: (b * (seq // ts) + i, 0)),
        out_shape=jax.ShapeDtypeStruct((batch * seq, D_MODEL), jnp.float32),
        scratch_shapes=[pltpu.VMEM((CONV_HALO + ts, CONV_CH), jnp.float32)],
        compiler_params=_params("parallel", "parallel"),
    )(h3, h3, dw_w.reshape(CONV_WIDTH, CONV_CH), dw_b.reshape(1, CONV_CH),
      ln_g.reshape(1, CONV_CH), ln_b.reshape(1, CONV_CH), w_pw_b)


def _attn_kernel(slopes_ref, q_ref, k_ref, v_ref, lq1_ref, lk1_ref, lq2_ref, lk2_ref, g_ref, o_ref, *, tq):
    h = pl.program_id(1)
    qi = pl.program_id(2)
    slope = slopes_ref[h]
    scale = ATTN_HEAD_DIM ** -0.5
    q = q_ref[0]
    lane = lax.broadcasted_iota(jnp.int32, q.shape, 1)
    zero = jnp.zeros_like(q)
    q1 = jnp.where(lane < ATTN_HEAD_DIM, q, zero)
    q2 = jnp.where(lane >= ATTN_HEAD_DIM, q, zero)
    qpos = qi * tq + lax.broadcasted_iota(jnp.int32, (tq, tq), 0)
    kcol = lax.broadcasted_iota(jnp.int32, (tq, tq), 1)

    def step(kb, carry):
        m1, l1, a1, m2, l2, a2 = carry
        k = k_ref[0, pl.ds(kb * tq, tq), :]
        v = v_ref[0, pl.ds(kb * tq, tq), :]
        dist = qpos - (kb * tq + kcol)
        bias = -slope * dist.astype(jnp.float32)
        keep = dist >= 0

        def one(qm, m, l, a):
            s = jnp.where(keep, _dot_nt(qm, k) * scale + bias, NEG_INF)
            m_new = jnp.maximum(m, jnp.max(s, axis=-1, keepdims=True))
            p = jnp.exp(s - m_new)
            corr = jnp.exp(m - m_new)
            l_new = corr * l + jnp.sum(p, axis=-1, keepdims=True)
            a_new = corr * a + _dot(p.astype(v.dtype), v)
            return m_new, l_new, a_new

        m1, l1, a1 = one(q1, m1, l1, a1)
        m2, l2, a2 = one(q2, m2, l2, a2)
        return m1, l1, a1, m2, l2, a2

    col = lambda val: jnp.full((tq, 1), val, jnp.float32)
    acc0 = jnp.zeros((tq, ATTN_V_DIM), jnp.float32)
    init = (col(NEG_INF), col(0.0), acc0, col(NEG_INF), col(0.0), acc0)
    m1, l1, a1, m2, l2, a2 = lax.fori_loop(0, qi + 1, step, init)

    lam = (jnp.exp(jnp.sum(lq1_ref[...] * lk1_ref[...], axis=-1, keepdims=True))
           - jnp.exp(jnp.sum(lq2_ref[...] * lk2_ref[...], axis=-1, keepdims=True)) + LAM_INIT)
    o = a1 / l1 - lam * (a2 / l2)
    ms = jnp.mean(o * o, axis=-1, keepdims=True)
    o = o * lax.rsqrt(ms + LN_EPS) * g_ref[...] * (1.0 - LAM_INIT)
    o_ref[0] = o.astype(o_ref.dtype)


def _diff_attention(qkv, lq1, lk1, lq2, lk2, subln_g, batch, seq):
    tq = 256
    qkv3 = qkv.reshape(batch, seq, COL_QKV)
    slopes = jnp.asarray(2.0 ** (-8.0 * np.arange(1, ATTN_HEADS + 1) / ATTN_HEADS), jnp.float32)
    hd = lambda: pl.BlockSpec((1, ATTN_HEAD_DIM), lambda b, h, i: (0, 0))
    return pl.pallas_call(
        functools.partial(_attn_kernel, tq=tq), name="diff_attention",
        grid=(batch, ATTN_HEADS, seq // tq),
        in_specs=[
            pl.BlockSpec(memory_space=pltpu.SMEM),
            pl.BlockSpec((1, tq, ATTN_V_DIM), lambda b, h, i: (b, i, h)),
            pl.BlockSpec((1, seq, ATTN_V_DIM), lambda b, h, i: (b, 0, ATTN_HEADS + h)),
            pl.BlockSpec((1, seq, ATTN_V_DIM), lambda b, h, i: (b, 0, 2 * ATTN_HEADS + h)),
            hd(), hd(), hd(), hd(),
            pl.BlockSpec((1, ATTN_V_DIM), lambda b, h, i: (0, 0)),
        ],
        out_specs=pl.BlockSpec((1, tq, ATTN_V_DIM), lambda b, h, i: (b, i, h)),
        out_shape=jax.ShapeDtypeStruct((batch, seq, ATTN_HEADS * ATTN_V_DIM), jnp.bfloat16),
        compiler_params=_params("parallel", "parallel", "arbitrary"),
    )(slopes, qkv3, qkv3, qkv3, lq1.reshape(1, -1), lk1.reshape(1, -1), lq2.reshape(1, -1),
      lk2.reshape(1, -1), subln_g.reshape(1, -1))


def _mix_kernel(gate_ref, yc_ref, on_ref, wo_ref, wout_ref, x_ref, g_ref, b_ref, x1_ref, x1b_ref):
    y_attn = _dot(on_ref[...], wo_ref[...])
    mixed = gate_ref[:, :D_MODEL] * yc_ref[...] + gate_ref[:, D_MODEL:] * y_attn
    z = _dot(mixed.astype(jnp.bfloat16), wout_ref[...])
    x1 = _layer_norm(ALPHA * x_ref[...] + z, g_ref[...], b_ref[...])
    x1_ref[...] = x1
    x1b_ref[...] = x1.astype(jnp.bfloat16)


def _mix(gates, y_conv, o_norm, w_o_b, w_out_b, x2d, ln_g, ln_b):
    t = x2d.shape[0]
    tm = 256
    row = lambda n: pl.BlockSpec((tm, n), lambda i: (i, 0))
    full = lambda a, b: pl.BlockSpec((a, b), lambda i: (0, 0))
    return pl.pallas_call(
        _mix_kernel, name="mix_deepnorm",
        grid=(t // tm,),
        in_specs=[row(COL_GATE), row(D_MODEL), row(ATTN_HEADS * ATTN_V_DIM),
                  full(ATTN_HEADS * ATTN_V_DIM, D_MODEL), full(D_MODEL, D_MODEL), row(D_MODEL),
                  full(1, D_MODEL), full(1, D_MODEL)],
        out_specs=[row(D_MODEL), row(D_MODEL)],
        out_shape=[jax.ShapeDtypeStruct((t, D_MODEL), jnp.float32),
                   jax.ShapeDtypeStruct((t, D_MODEL), jnp.bfloat16)],
        compiler_params=_params("parallel"),
    )(gates, y_conv, o_norm, w_o_b, w_out_b, x2d, ln_g.reshape(1, -1), ln_b.reshape(1, -1))


def _peer_prep_kernel(x_ref, wq_ref, keys_ref, s1_ref, e1_ref, s2_ref, e2_ref, tau_ref, top_ref, *, tm):
    q = _dot(x_ref[...], wq_ref[...]).astype(jnp.bfloat16)
    s_refs = (s1_ref, s2_ref)
    for c in range(2):
        for h in range(PEER_HEADS):
            idx = c * PEER_HEADS + h
            s = _dot_nt(keys_ref[idx], q[:, idx * PEER_HALF:(idx + 1) * PEER_HALF])
            s_refs[c][h] = s
            work = s
            for r in range(PEER_TOPK):
                mx = jnp.max(work, axis=0, keepdims=True)
                top_ref[c, r, pl.ds(h, 1), :] = mx
                work = jnp.where(work == mx, NEG_INF, work)
    cands = [top_ref[0, i] + top_ref[1, j] for (i, j) in CAND_PAIRS]
    work = list(cands)
    remaining = jnp.full((PEER_HEADS, tm), float(PEER_TOPK), jnp.float32)
    tau = jnp.full((PEER_HEADS, tm), NEG_INF, jnp.float32)
    for _ in range(PEER_TOPK):
        mx = functools.reduce(jnp.maximum, work)
        hit = [w == mx for w in work]
        cnt = functools.reduce(jnp.add, [hm.astype(jnp.float32) for hm in hit])
        newly = jnp.logical_and(remaining > 0.0, cnt >= remaining)
        tau = jnp.where(newly, mx, tau)
        remaining = remaining - cnt
        work = [jnp.where(hm, NEG_INF, w) for hm, w in zip(hit, work)]
    tau_ref[...] = tau
    m1 = top_ref[0, 0]
    m2 = top_ref[1, 0]
    ex1 = [jnp.exp(top_ref[0, i] - m1) for i in range(PEER_TOPK)]
    ex2 = [jnp.exp(top_ref[1, j] - m2) for j in range(PEER_TOPK)]
    z = jnp.zeros((PEER_HEADS, tm), jnp.float32)
    for cand, (i, j) in zip(cands, CAND_PAIRS):
        z = z + jnp.where(cand >= tau, ex1[i] * ex2[j], 0.0)
    for h in range(PEER_HEADS):
        e1_ref[h] = jnp.exp(s1_ref[h] - m1[h:h + 1, :]) / z[h:h + 1, :]
        e2_ref[h] = jnp.exp(s2_ref[h] - m2[h:h + 1, :])


def _peer_prep(x1b, wq_b, keys_b):
    t = x1b.shape[0]
    tm = 256
    nq = 2 * PEER_HEADS * PEER_HALF
    big = lambda: pl.BlockSpec((PEER_HEADS, PEER_N_KEYS, tm), lambda i: (0, 0, i))
    big_shape = jax.ShapeDtypeStruct((PEER_HEADS, PEER_N_KEYS, t), jnp.float32)
    return pl.pallas_call(
        functools.partial(_peer_prep_kernel, tm=tm), name="peer_prep",
        grid=(t // tm,),
        in_specs=[pl.BlockSpec((tm, D_MODEL), lambda i: (i, 0)),
                  pl.BlockSpec((D_MODEL, nq), lambda i: (0, 0)),
                  pl.BlockSpec((2 * PEER_HEADS, PEER_N_KEYS, PEER_HALF), lambda i: (0, 0, 0))],
        out_specs=[big(), big(), big(), big(), pl.BlockSpec((PEER_HEADS, tm), lambda i: (0, i))],
        out_shape=[big_shape, big_shape, big_shape, big_shape,
                   jax.ShapeDtypeStruct((PEER_HEADS, t), jnp.float32)],
        scratch_shapes=[pltpu.VMEM((2, PEER_TOPK, PEER_HEADS, tm), jnp.float32)],
        compiler_params=_params("parallel"),
    )(x1b, wq_b, keys_b)


def _gelu_tanh(x):
    return 0.5 * x * (1.0 + jnp.tanh(math.sqrt(2.0 / math.pi) * (x + 0.044715 * (x * x * x))))


def _peer_dense_kernel(xb_ref, u_ref, v_ref, s1_ref, e1_ref, s2_ref, e2_ref, tau_ref, x1_ref, g_ref, b_ref,
                       x2_ref, x2b_ref, acc_ref, wd_ref, *, tm, te):
    j = pl.program_id(1)

    @pl.when(j == 0)
    def _():
        acc_ref[...] = jnp.zeros_like(acc_ref)

    hid = _dot_nt(u_ref[...], xb_ref[...])
    na = te // PEER_N_KEYS
    for k in range(na):
        a = j * na + k
        gate = jnp.zeros((PEER_N_KEYS, tm), jnp.float32)
        for h in range(PEER_HEADS):
            pair = s1_ref[h, pl.ds(a, 1), :] + s2_ref[h]
            sel = pair >= tau_ref[pl.ds(h, 1), :]
            gate = gate + jnp.where(sel, e1_ref[h, pl.ds(a, 1), :] * e2_ref[h], 0.0)
        hk = hid[k * PEER_N_KEYS:(k + 1) * PEER_N_KEYS, :]
        wd_ref[pl.ds(k * PEER_N_KEYS, PEER_N_KEYS), :] = (gate * _gelu_tanh(hk)).astype(jnp.bfloat16)
    acc_ref[...] += _dot_tn(wd_ref[...], v_ref[...])

    @pl.when(j == pl.num_programs(1) - 1)
    def _():
        x2 = _layer_norm(ALPHA * x1_ref[...] + acc_ref[...], g_ref[...], b_ref[...])
        x2_ref[...] = x2
        x2b_ref[...] = x2.astype(jnp.bfloat16)


def _peer_dense(x1, x1b, u_b, v_b, s1, e1, s2, e2, tau, ln_g, ln_b):
    t = x1.shape[0]
    n_exp = u_b.shape[0]
    tm, te = 256, 512
    big = lambda: pl.BlockSpec((PEER_HEADS, PEER_N_KEYS, tm), lambda i, j: (0, 0, i))
    row = lambda: pl.BlockSpec((tm, D_MODEL), lambda i, j: (i, 0))
    tab = lambda: pl.BlockSpec((te, D_MODEL), lambda i, j: (j, 0))
    vec = lambda: pl.BlockSpec((1, D_MODEL), lambda i, j: (0, 0))
    return pl.pallas_call(
        functools.partial(_peer_dense_kernel, tm=tm, te=te), name="peer_dense",
        grid=(t // tm, n_exp // te),
        in_specs=[row(), tab(), tab(), big(), big(), big(), big(),
                  pl.BlockSpec((PEER_HEADS, tm), lambda i, j: (0, i)), row(), vec(), vec()],
        out_specs=[row(), row()],
        out_shape=[jax.ShapeDtypeStruct((t, D_MODEL), jnp.float32),
                   jax.ShapeDtypeStruct((t, D_MODEL), jnp.bfloat16)],
        scratch_shapes=[pltpu.VMEM((tm, D_MODEL), jnp.float32), pltpu.VMEM((te, tm), jnp.bfloat16)],
        compiler_params=_params("parallel", "arbitrary"),
    )(x1b, u_b, v_b, s1, e1, s2, e2, tau, x1, ln_g.reshape(1, -1), ln_b.reshape(1, -1))


def _ple_kernel(xb_ref, wg_ref, bg_ref, p_ref, wp_ref, x_ref, o_ref):
    gate = jax.nn.sigmoid(_dot(xb_ref[...], wg_ref[...]) + bg_ref[...])
    o_ref[...] = x_ref[...] + gate * _dot(p_ref[...], wp_ref[...])


def _ple(x2, x2b, w_gate_b, b_gate, p_b, w_proj_b):
    t = x2.shape[0]
    tm, tn = 512, 512
    return pl.pallas_call(
        _ple_kernel, name="ple",
        grid=(t // tm, D_MODEL // tn),
        in_specs=[pl.BlockSpec((tm, D_MODEL), lambda i, j: (i, 0)),
                  pl.BlockSpec((D_MODEL, tn), lambda i, j: (0, j)),
                  pl.BlockSpec((1, tn), lambda i, j: (0, j)),
                  pl.BlockSpec((tm, P_DIM), lambda i, j: (i, 0)),
                  pl.BlockSpec((P_DIM, tn), lambda i, j: (0, j)),
                  pl.BlockSpec((tm, tn), lambda i, j: (i, j))],
        out_specs=pl.BlockSpec((tm, tn), lambda i, j: (i, j)),
        out_shape=jax.ShapeDtypeStruct((t, D_MODEL), jnp.float32),
        compiler_params=_params("parallel", "parallel"),
    )(x2b, w_gate_b, b_gate.reshape(1, -1), p_b, w_proj_b, x2)


def kernel(x, p, w_in, b_gate, conv_dw_w, conv_dw_b, conv_ln_g, conv_ln_b, conv_w_out, attn_lambda_q1, attn_lambda_k1, attn_lambda_q2, attn_lambda_k2, attn_subln_g, attn_w_o, w_out, ln1_g, ln1_b, peer_w_q, peer_sub_keys, peer_u, peer_v, ln2_g, ln2_b, ple_w_proj, ple_w_gate, ple_b_gate):
    batch, seq, d = x.shape
    t = batch * seq
    bf = jnp.bfloat16
    assert w_in.shape[0] == DEPTH == 1 and d == D_MODEL
    x2d = x.reshape(t, d)

    glu, qkv, gates = _input_projection(x2d.astype(bf), w_in[0].astype(bf), b_gate[0])
    y_conv = _conformer_conv(glu, conv_dw_w[0], conv_dw_b[0], conv_ln_g[0], conv_ln_b[0],
                             conv_w_out[0].astype(bf), batch, seq)
    o_norm = _diff_attention(qkv, attn_lambda_q1[0], attn_lambda_k1[0], attn_lambda_q2[0],
                             attn_lambda_k2[0], attn_subln_g[0], batch, seq)
    x1, x1b = _mix(gates, y_conv, o_norm.reshape(t, -1), attn_w_o[0].astype(bf), w_out[0].astype(bf),
                   x2d, ln1_g[0], ln1_b[0])

    wq = peer_w_q[0].reshape(d, PEER_HEADS, 2, PEER_HALF).transpose(0, 2, 1, 3).reshape(d, -1).astype(bf)
    keys = peer_sub_keys[0].transpose(1, 0, 2, 3).reshape(2 * PEER_HEADS, PEER_N_KEYS, PEER_HALF).astype(bf)
    s1, e1, s2, e2, tau = _peer_prep(x1b, wq, keys)
    x2, x2b = _peer_dense(x1, x1b, peer_u[0].astype(bf), peer_v[0].astype(bf), s1, e1, s2, e2, tau,
                          ln2_g[0], ln2_b[0])

    out = _ple(x2, x2b, ple_w_gate[0].astype(bf), ple_b_gate[0], p[0].reshape(t, P_DIM).astype(bf),
               ple_w_proj[0].astype(bf))
    return out.reshape(batch, seq, d)
```

```python
import functools
import math

import jax
import jax.numpy as jnp
import numpy as np
from jax import lax
from jax.experimental import pallas as pl
from jax.experimental.pallas import tpu as pltpu

D_MODEL = 2048
CONV_CH = 1024
CONV_WIDTH = 31
ATTN_HEADS = 8
ATTN_HEAD_DIM = 64
ATTN_V_DIM = 2 * ATTN_HEAD_DIM
PEER_HEADS = 8
PEER_N_KEYS = 128
PEER_HALF = 128
PEER_TOPK = 16
P_DIM = 256
LN_EPS = 1e-5
DEPTH = 1
ALPHA = (2.0 * DEPTH) ** 0.25
LAM_INIT = 0.8 - 0.6 * math.exp(-0.3 * 0)

COL_CONV = 2 * CONV_CH
COL_QKV = 3 * ATTN_HEADS * ATTN_V_DIM
COL_GATE = 2 * D_MODEL

VMEM_LIMIT_BYTES = 56 * 1024 * 1024
NEG_INF = float("-inf")

CAND_PAIRS = tuple((i, j) for i in range(PEER_TOPK) for j in range(PEER_TOPK)
                   if (i + 1) * (j + 1) <= PEER_TOPK)


def _params(*sem, **kw):
    return pltpu.CompilerParams(dimension_semantics=sem, vmem_limit_bytes=VMEM_LIMIT_BYTES, **kw)


def _dot(a, b):
    return jnp.dot(a, b, preferred_element_type=jnp.float32)


def _dot_nt(a, b):
    return lax.dot_general(a, b, (((1,), (1,)), ((), ())), preferred_element_type=jnp.float32)


def _dot_tn(a, b):
    return lax.dot_general(a, b, (((0,), (0,)), ((), ())), preferred_element_type=jnp.float32)


def _layer_norm(r, g, b):
    mu = jnp.mean(r, axis=-1, keepdims=True)
    c = r - mu
    var = jnp.mean(c * c, axis=-1, keepdims=True)
    return c * lax.rsqrt(var + LN_EPS) * g + b


def _proj_glu_kernel(x_ref, wa_ref, wb_ref, o_ref):
    x = x_ref[...]
    o_ref[...] = _dot(x, wa_ref[...]) * jax.nn.sigmoid(_dot(x, wb_ref[...]))


def _proj_qkv_kernel(x_ref, w_ref, o_ref):
    o_ref[...] = _dot(x_ref[...], w_ref[...]).astype(o_ref.dtype)


def _proj_gate_kernel(x_ref, w_ref, b_ref, o_ref):
    o_ref[...] = jax.nn.sigmoid(_dot(x_ref[...], w_ref[...]) + b_ref[...])


def _input_projection(xb, w_in_b, b_gate):
    t = xb.shape[0]
    tm = 1024
    x_spec = pl.BlockSpec((tm, D_MODEL), lambda i, j: (i, 0))

    def w_spec(col0, tn):
        return pl.BlockSpec((D_MODEL, tn), lambda i, j: (0, j + col0 // tn))

    def o_spec(tn):
        return pl.BlockSpec((tm, tn), lambda i, j: (i, j))

    tn = 512
    glu = pl.pallas_call(
        _proj_glu_kernel, name="proj_glu",
        grid=(t // tm, CONV_CH // tn),
        in_specs=[x_spec, w_spec(0, tn), w_spec(CONV_CH, tn)],
        out_specs=o_spec(tn),
        out_shape=jax.ShapeDtypeStruct((t, CONV_CH), jnp.float32),
        compiler_params=_params("parallel", "parallel"),
    )(xb, w_in_b, w_in_b)
    tn = 1024
    qkv = pl.pallas_call(
        _proj_qkv_kernel, name="proj_qkv",
        grid=(t // tm, COL_QKV // tn),
        in_specs=[x_spec, w_spec(COL_CONV, tn)],
        out_specs=o_spec(tn),
        out_shape=jax.ShapeDtypeStruct((t, COL_QKV), jnp.bfloat16),
        compiler_params=_params("parallel", "parallel"),
    )(xb, w_in_b)
    gates = pl.pallas_call(
        _proj_gate_kernel, name="proj_gate",
        grid=(t // tm, COL_GATE // tn),
        in_specs=[x_spec, w_spec(COL_CONV + COL_QKV, tn), pl.BlockSpec((1, tn), lambda i, j: (0, j))],
        out_specs=o_spec(tn),
        out_shape=jax.ShapeDtypeStruct((t, COL_GATE), jnp.float32),
        compiler_params=_params("parallel", "parallel"),
    )(xb, w_in_b, b_gate.reshape(1, COL_GATE))
    return glu, qkv, gates


CONV_HALO = 32


def _conv_kernel(prev_ref, cur_ref, dww_ref, dwb_ref, lng_ref, lnb_ref, wpw_ref, o_ref, ext_ref, *, ts):
    i = pl.program_id(1)
    halo = prev_ref[0]
    ext_ref[pl.ds(0, CONV_HALO), :] = jnp.where(i > 0, halo, jnp.zeros_like(halo))
    ext_ref[pl.ds(CONV_HALO, ts), :] = cur_ref[0]
    off0 = CONV_HALO - (CONV_WIDTH - 1)
    acc = jnp.zeros((ts, CONV_CH), jnp.float32)
    for j in range(CONV_WIDTH):
        acc = acc + ext_ref[pl.ds(off0 + j, ts), :] * dww_ref[pl.ds(j, 1), :]
    acc = acc + dwb_ref[...]
    y = _layer_norm(acc, lng_ref[...], lnb_ref[...])
    y = y * jax.nn.sigmoid(y)
    o_ref[...] = _dot(y.astype(jnp.bfloat16), wpw_ref[...])


def _conformer_conv(glu, dw_w, dw_b, ln_g, ln_b, w_pw_b, batch, seq):
    ts = 256
    h3 = glu.reshape(batch, seq, CONV_CH)
    nblk = ts // CONV_HALO
    vec = lambda: pl.BlockSpec((1, CONV_CH), lambda b, i: (0, 0))
    return pl.pallas_call(
        functools.partial(_conv_kernel, ts=ts), name="conv_module",
        grid=(batch, seq // ts),
        in_specs=[
            pl.BlockSpec((1, CONV_HALO, CONV_CH), lambda b, i: (b, jnp.maximum(i * nblk - 1, 0), 0)),
            pl.BlockSpec((1, ts, CONV_CH), lambda b, i: (b, i, 0)),
            pl.BlockSpec((CONV_WIDTH, CONV_CH), lambda b, i: (0, 0)),
            vec(), vec(), vec(),
            pl.BlockSpec((CONV_CH, D_MODEL), lambda b, i: (0, 0)),
        ],
        out_specs=pl.BlockSpec((ts, D_MODEL), lambda b, i: (b * (seq // ts) + i, 0)),
        out_shape=jax.ShapeDtypeStruct((batch * seq, D_MODEL), jnp.float32),
        scratch_shapes=[pltpu.VMEM((CONV_HALO + ts, CONV_CH), jnp.float32)],
        compiler_params=_params("parallel", "parallel"),
    )(h3, h3, dw_w.reshape(CONV_WIDTH, CONV_CH), dw_b.reshape(1, CONV_CH),
      ln_g.reshape(1, CONV_CH), ln_b.reshape(1, CONV_CH), w_pw_b)


def _attn_kernel(slopes_ref, q_ref, k_ref, v_ref, lq1_ref, lk1_ref, lq2_ref, lk2_ref, g_ref, o_ref, *, tq):
    h = pl.program_id(1)
    qi = pl.program_id(2)
    slope = slopes_ref[h]
    q = q_ref[0] * (ATTN_HEAD_DIM ** -0.5)
    lane = lax.broadcasted_iota(jnp.int32, q.shape, 1)
    zero = jnp.zeros_like(q)
    q1 = jnp.where(lane < ATTN_HEAD_DIM, q, zero)
    q2 = jnp.where(lane >= ATTN_HEAD_DIM, q, zero)
    causal = (lax.broadcasted_iota(jnp.int32, (tq, tq), 0)
              >= lax.broadcasted_iota(jnp.int32, (tq, tq), 1))
    kcol = lax.broadcasted_iota(jnp.int32, (1, tq), 1)

    def block(kb, carry, diagonal):
        m1, l1, a1, m2, l2, a2 = carry
        k = k_ref[0, pl.ds(kb * tq, tq), :]
        v = v_ref[0, pl.ds(kb * tq, tq), :]
        bias = slope * ((kb - qi) * tq + kcol).astype(jnp.float32)

        def one(qm, m, l, a):
            s = _dot_nt(qm, k) + bias
            if diagonal:
                s = jnp.where(causal, s, NEG_INF)
            m_new = jnp.maximum(m, jnp.max(s, axis=-1, keepdims=True))
            p = jnp.exp(s - m_new)
            corr = jnp.exp(m - m_new)
            l_new = corr * l + jnp.sum(p, axis=-1, keepdims=True)
            a_new = corr * a + _dot(p.astype(v.dtype), v)
            return m_new, l_new, a_new

        m1, l1, a1 = one(q1, m1, l1, a1)
        m2, l2, a2 = one(q2, m2, l2, a2)
        return m1, l1, a1, m2, l2, a2

    col = lambda val: jnp.full((tq, 1), val, jnp.float32)
    acc0 = jnp.zeros((tq, ATTN_V_DIM), jnp.float32)
    init = (col(NEG_INF), col(0.0), acc0, col(NEG_INF), col(0.0), acc0)
    carry = lax.fori_loop(0, qi, functools.partial(block, diagonal=False), init)
    m1, l1, a1, m2, l2, a2 = block(qi, carry, diagonal=True)

    lam = (jnp.exp(jnp.sum(lq1_ref[...] * lk1_ref[...], axis=-1, keepdims=True))
           - jnp.exp(jnp.sum(lq2_ref[...] * lk2_ref[...], axis=-1, keepdims=True)) + LAM_INIT)
    o = a1 / l1 - lam * (a2 / l2)
    ms = jnp.mean(o * o, axis=-1, keepdims=True)
    o = o * lax.rsqrt(ms + LN_EPS) * g_ref[...] * (1.0 - LAM_INIT)
    o_ref[0] = o.astype(o_ref.dtype)


def _diff_attention(qkv, lq1, lk1, lq2, lk2, subln_g, batch, seq):
    tq = 512
    qkv3 = qkv.reshape(batch, seq, COL_QKV)
    slopes = jnp.asarray(2.0 ** (-8.0 * np.arange(1, ATTN_HEADS + 1) / ATTN_HEADS), jnp.float32)
    hd = lambda: pl.BlockSpec((1, ATTN_HEAD_DIM), lambda b, h, i: (0, 0))
    return pl.pallas_call(
        functools.partial(_attn_kernel, tq=tq), name="diff_attention",
        grid=(batch, ATTN_HEADS, seq // tq),
        in_specs=[
            pl.BlockSpec(memory_space=pltpu.SMEM),
            pl.BlockSpec((1, tq, ATTN_V_DIM), lambda b, h, i: (b, i, h)),
            pl.BlockSpec((1, seq, ATTN_V_DIM), lambda b, h, i: (b, 0, ATTN_HEADS + h)),
            pl.BlockSpec((1, seq, ATTN_V_DIM), lambda b, h, i: (b, 0, 2 * ATTN_HEADS + h)),
            hd(), hd(), hd(), hd(),
            pl.BlockSpec((1, ATTN_V_DIM), lambda b, h, i: (0, 0)),
        ],
        out_specs=pl.BlockSpec((1, tq, ATTN_V_DIM), lambda b, h, i: (b, i, h)),
        out_shape=jax.ShapeDtypeStruct((batch, seq, ATTN_HEADS * ATTN_V_DIM), jnp.bfloat16),
        compiler_params=_params("parallel", "parallel", "arbitrary"),
    )(slopes, qkv3, qkv3, qkv3, lq1.reshape(1, -1), lk1.reshape(1, -1), lq2.reshape(1, -1),
      lk2.reshape(1, -1), subln_g.reshape(1, -1))


def _mix_kernel(gate_ref, yc_ref, on_ref, wo_ref, wout_ref, x_ref, g_ref, b_ref, x1_ref, x1b_ref):
    y_attn = _dot(on_ref[...], wo_ref[...])
    mixed = gate_ref[:, :D_MODEL] * yc_ref[...] + gate_ref[:, D_MODEL:] * y_attn
    z = _dot(mixed.astype(jnp.bfloat16), wout_ref[...])
    x1 = _layer_norm(ALPHA * x_ref[...] + z, g_ref[...], b_ref[...])
    x1_ref[...] = x1
    x1b_ref[...] = x1.astype(jnp.bfloat16)


def _mix(gates, y_conv, o_norm, w_o_b, w_out_b, x2d, ln_g, ln_b):
    t = x2d.shape[0]
    tm = 256
    row = lambda n: pl.BlockSpec((tm, n), lambda i: (i, 0))
    full = lambda a, b: pl.BlockSpec((a, b), lambda i: (0, 0))
    return pl.pallas_call(
        _mix_kernel, name="mix_deepnorm",
        grid=(t // tm,),
        in_specs=[row(COL_GATE), row(D_MODEL), row(ATTN_HEADS * ATTN_V_DIM),
                  full(ATTN_HEADS * ATTN_V_DIM, D_MODEL), full(D_MODEL, D_MODEL), row(D_MODEL),
                  full(1, D_MODEL), full(1, D_MODEL)],
        out_specs=[row(D_MODEL), row(D_MODEL)],
        out_shape=[jax.ShapeDtypeStruct((t, D_MODEL), jnp.float32),
                   jax.ShapeDtypeStruct((t, D_MODEL), jnp.bfloat16)],
        compiler_params=_params("parallel"),
    )(gates, y_conv, o_norm, w_o_b, w_out_b, x2d, ln_g.reshape(1, -1), ln_b.reshape(1, -1))


def _peer_prep_kernel(x_ref, wq_ref, keys_ref, s1_ref, e1_ref, s2_ref, e2_ref, tau_ref, top_ref, *, tm):
    q = _dot(x_ref[...], wq_ref[...]).astype(jnp.bfloat16)
    s_refs = (s1_ref, s2_ref)
    for c in range(2):
        for h in range(PEER_HEADS):
            idx = c * PEER_HEADS + h
            s = _dot_nt(keys_ref[idx], q[:, idx * PEER_HALF:(idx + 1) * PEER_HALF])
            s_refs[c][h] = s
            work = s
            for r in range(PEER_TOPK):
                mx = jnp.max(work, axis=0, keepdims=True)
                top_ref[c, r, pl.ds(h, 1), :] = mx
                work = jnp.where(work == mx, NEG_INF, work)
    cands = [top_ref[0, i] + top_ref[1, j] for (i, j) in CAND_PAIRS]
    work = list(cands)
    remaining = jnp.full((PEER_HEADS, tm), float(PEER_TOPK), jnp.float32)
    tau = jnp.full((PEER_HEADS, tm), NEG_INF, jnp.float32)
    for _ in range(PEER_TOPK):
        mx = functools.reduce(jnp.maximum, work)
        hit = [w == mx for w in work]
        cnt = functools.reduce(jnp.add, [hm.astype(jnp.float32) for hm in hit])
        newly = jnp.logical_and(remaining > 0.0, cnt >= remaining)
        tau = jnp.where(newly, mx, tau)
        remaining = remaining - cnt
        work = [jnp.where(hm, NEG_INF, w) for hm, w in zip(hit, work)]
    tau_ref[...] = tau
    m1 = top_ref[0, 0]
    m2 = top_ref[1, 0]
    ex1 = [jnp.exp(top_ref[0, i] - m1) for i in range(PEER_TOPK)]
    ex2 = [jnp.exp(top_ref[1, j] - m2) for j in range(PEER_TOPK)]
    z = jnp.zeros((PEER_HEADS, tm), jnp.float32)
    for cand, (i, j) in zip(cands, CAND_PAIRS):
        z = z + jnp.where(cand >= tau, ex1[i] * ex2[j], 0.0)
    for h in range(PEER_HEADS):
        e1_ref[h] = jnp.exp(s1_ref[h] - m1[h:h + 1, :]) / z[h:h + 1, :]
        e2_ref[h] = jnp.exp(s2_ref[h] - m2[h:h + 1, :])


def _peer_prep(x1b, wq_b, keys_b):
    t = x1b.shape[0]
    tm = 256
    nq = 2 * PEER_HEADS * PEER_HALF
    big = lambda: pl.BlockSpec((PEER_HEADS, PEER_N_KEYS, tm), lambda i: (0, 0, i))
    big_shape = jax.ShapeDtypeStruct((PEER_HEADS, PEER_N_KEYS, t), jnp.float32)
    return pl.pallas_call(
        functools.partial(_peer_prep_kernel, tm=tm), name="peer_prep",
        grid=(t // tm,),
        in_specs=[pl.BlockSpec((tm, D_MODEL), lambda i: (i, 0)),
                  pl.BlockSpec((D_MODEL, nq), lambda i: (0, 0)),
                  pl.BlockSpec((2 * PEER_HEADS, PEER_N_KEYS, PEER_HALF), lambda i: (0, 0, 0))],
        out_specs=[big(), big(), big(), big(), pl.BlockSpec((PEER_HEADS, tm), lambda i: (0, i))],
        out_shape=[big_shape, big_shape, big_shape, big_shape,
                   jax.ShapeDtypeStruct((PEER_HEADS, t), jnp.float32)],
        scratch_shapes=[pltpu.VMEM((2, PEER_TOPK, PEER_HEADS, tm), jnp.float32)],
        compiler_params=_params("parallel"),
    )(x1b, wq_b, keys_b)


def _gelu_tanh(x):
    return 0.5 * x * (1.0 + jnp.tanh(math.sqrt(2.0 / math.pi) * (x + 0.044715 * (x * x * x))))


def _peer_dense_kernel(xb_ref, u_ref, v_ref, s1_ref, e1_ref, s2_ref, e2_ref, tau_ref, o_ref, acc_ref, wd_ref,
                       *, tm, te):
    j = pl.program_id(1)
    n_tiles = pl.num_programs(1) - 1
    jc = jnp.minimum(j, n_tiles - 1)

    @pl.when(j == 0)
    def _():
        acc_ref[...] = jnp.zeros_like(acc_ref)
        wd_ref[...] = jnp.zeros_like(wd_ref)

    acc_ref[...] += _dot_tn(wd_ref[...], v_ref[...])
    hid = _dot_nt(u_ref[...], xb_ref[...])
    na = te // PEER_N_KEYS
    for k in range(na):
        a = jc * na + k
        gate = jnp.zeros((PEER_N_KEYS, tm), jnp.float32)
        for h in range(PEER_HEADS):
            pair = s1_ref[h, pl.ds(a, 1), :] + s2_ref[h]
            sel = pair >= tau_ref[pl.ds(h, 1), :]
            gate = gate + jnp.where(sel, e1_ref[h, pl.ds(a, 1), :] * e2_ref[h], 0.0)
        hk = hid[k * PEER_N_KEYS:(k + 1) * PEER_N_KEYS, :]
        wd_ref[pl.ds(k * PEER_N_KEYS, PEER_N_KEYS), :] = (gate * _gelu_tanh(hk)).astype(jnp.bfloat16)

    @pl.when(j == n_tiles)
    def _():
        o_ref[...] = acc_ref[...]


def _peer_dense(x1b, u_b, v_b, s1, e1, s2, e2, tau):
    t = x1b.shape[0]
    n_exp = u_b.shape[0]
    tm, te = 512, 512
    n_tiles = n_exp // te
    big = lambda: pl.BlockSpec((PEER_HEADS, PEER_N_KEYS, tm), lambda i, j: (0, 0, i))
    row = lambda: pl.BlockSpec((tm, D_MODEL), lambda i, j: (i, 0))
    return pl.pallas_call(
        functools.partial(_peer_dense_kernel, tm=tm, te=te), name="peer_dense",
        grid=(t // tm, n_tiles + 1),
        in_specs=[row(),
                  pl.BlockSpec((te, D_MODEL), lambda i, j: (jnp.minimum(j, n_tiles - 1), 0)),
                  pl.BlockSpec((te, D_MODEL), lambda i, j: (jnp.maximum(j - 1, 0), 0)),
                  big(), big(), big(), big(),
                  pl.BlockSpec((PEER_HEADS, tm), lambda i, j: (0, i))],
        out_specs=row(),
        out_shape=jax.ShapeDtypeStruct((t, D_MODEL), jnp.float32),
        scratch_shapes=[pltpu.VMEM((tm, D_MODEL), jnp.float32), pltpu.VMEM((te, tm), jnp.bfloat16)],
        compiler_params=_params("parallel", "arbitrary"),
    )(x1b, u_b, v_b, s1, e1, s2, e2, tau)


def _ple_kernel(x1_ref, po_ref, g_ref, b_ref, wg_ref, bg_ref, p_ref, wp_ref, o_ref, x2_ref, x2b_ref, *, tn):
    j = pl.program_id(1)

    @pl.when(j == 0)
    def _():
        x2 = _layer_norm(ALPHA * x1_ref[...] + po_ref[...], g_ref[...], b_ref[...])
        x2_ref[...] = x2
        x2b_ref[...] = x2.astype(jnp.bfloat16)

    gate = jax.nn.sigmoid(_dot(x2b_ref[...], wg_ref[...]) + bg_ref[...])
    col = pl.multiple_of(j * tn, tn)
    o_ref[...] = x2_ref[:, pl.ds(col, tn)] + gate * _dot(p_ref[...], wp_ref[...])


def _ple(x1, peer_out, ln_g, ln_b, w_gate_b, b_gate, p_b, w_proj_b):
    t = x1.shape[0]
    tm, tn = 512, 512
    row = lambda: pl.BlockSpec((tm, D_MODEL), lambda i, j: (i, 0))
    vec = lambda: pl.BlockSpec((1, D_MODEL), lambda i, j: (0, 0))
    return pl.pallas_call(
        functools.partial(_ple_kernel, tn=tn), name="ple",
        grid=(t // tm, D_MODEL // tn),
        in_specs=[row(), row(), vec(), vec(),
                  pl.BlockSpec((D_MODEL, tn), lambda i, j: (0, j)),
                  pl.BlockSpec((1, tn), lambda i, j: (0, j)),
                  pl.BlockSpec((tm, P_DIM), lambda i, j: (i, 0)),
                  pl.BlockSpec((P_DIM, tn), lambda i, j: (0, j))],
        out_specs=pl.BlockSpec((tm, tn), lambda i, j: (i, j)),
        out_shape=jax.ShapeDtypeStruct((t, D_MODEL), jnp.float32),
        scratch_shapes=[pltpu.VMEM((tm, D_MODEL), jnp.float32), pltpu.VMEM((tm, D_MODEL), jnp.bfloat16)],
        compiler_params=_params("parallel", "arbitrary"),
    )(x1, peer_out, ln_g.reshape(1, -1), ln_b.reshape(1, -1), w_gate_b, b_gate.reshape(1, -1), p_b, w_proj_b)


def kernel(x, p, w_in, b_gate, conv_dw_w, conv_dw_b, conv_ln_g, conv_ln_b, conv_w_out, attn_lambda_q1, attn_lambda_k1, attn_lambda_q2, attn_lambda_k2, attn_subln_g, attn_w_o, w_out, ln1_g, ln1_b, peer_w_q, peer_sub_keys, peer_u, peer_v, ln2_g, ln2_b, ple_w_proj, ple_w_gate, ple_b_gate):
    batch, seq, d = x.shape
    t = batch * seq
    bf = jnp.bfloat16
    assert w_in.shape[0] == DEPTH == 1 and d == D_MODEL
    x2d = x.reshape(t, d)

    glu, qkv, gates = _input_projection(x2d.astype(bf), w_in[0].astype(bf), b_gate[0])
    y_conv = _conformer_conv(glu, conv_dw_w[0], conv_dw_b[0], conv_ln_g[0], conv_ln_b[0],
                             conv_w_out[0].astype(bf), batch, seq)
    o_norm = _diff_attention(qkv, attn_lambda_q1[0], attn_lambda_k1[0], attn_lambda_q2[0],
                             attn_lambda_k2[0], attn_subln_g[0], batch, seq)
    x1, x1b = _mix(gates, y_conv, o_norm.reshape(t, -1), attn_w_o[0].astype(bf), w_out[0].astype(bf),
                   x2d, ln1_g[0], ln1_b[0])

    wq = peer_w_q[0].reshape(d, PEER_HEADS, 2, PEER_HALF).transpose(0, 2, 1, 3).reshape(d, -1).astype(bf)
    keys = peer_sub_keys[0].transpose(1, 0, 2, 3).reshape(2 * PEER_HEADS, PEER_N_KEYS, PEER_HALF).astype(bf)
    s1, e1, s2, e2, tau = _peer_prep(x1b, wq, keys)
    peer_out = _peer_dense(x1b, peer_u[0].astype(bf), peer_v[0].astype(bf), s1, e1, s2, e2, tau)

    out = _ple(x1, peer_out, ln2_g[0], ln2_b[0], ple_w_gate[0].astype(bf), ple_b_gate[0],
               p[0].reshape(t, P_DIM).astype(bf), ple_w_proj[0].astype(bf))
    return out.reshape(batch, seq, d)
```

```python
import functools
import math

import jax
import jax.numpy as jnp
import numpy as np
from jax import lax
from jax.experimental import pallas as pl
from jax.experimental.pallas import tpu as pltpu

D_MODEL = 2048
CONV_CH = 1024
CONV_WIDTH = 31
ATTN_HEADS = 8
ATTN_HEAD_DIM = 64
ATTN_V_DIM = 2 * ATTN_HEAD_DIM
PEER_HEADS = 8
PEER_N_KEYS = 128
PEER_HALF = 128
PEER_TOPK = 16
P_DIM = 256
LN_EPS = 1e-5
DEPTH = 1
ALPHA = (2.0 * DEPTH) ** 0.25
LAM_INIT = 0.8 - 0.6 * math.exp(-0.3 * 0)

COL_CONV = 2 * CONV_CH
COL_QKV = 3 * ATTN_HEADS * ATTN_V_DIM
COL_GATE = 2 * D_MODEL

VMEM_LIMIT_BYTES = 56 * 1024 * 1024
NEG_INF = float("-inf")

CAND_PAIRS = tuple((i, j) for i in range(PEER_TOPK) for j in range(PEER_TOPK)
                   if (i + 1) * (j + 1) <= PEER_TOPK)


def _params(*sem, **kw):
    return pltpu.CompilerParams(dimension_semantics=sem, vmem_limit_bytes=VMEM_LIMIT_BYTES, **kw)


def _dot(a, b):
    return jnp.dot(a, b, preferred_element_type=jnp.float32)


def _dot_nt(a, b):
    return lax.dot_general(a, b, (((1,), (1,)), ((), ())), preferred_element_type=jnp.float32)


def _dot_tn(a, b):
    return lax.dot_general(a, b, (((0,), (0,)), ((), ())), preferred_element_type=jnp.float32)


def _layer_norm(r, g, b):
    mu = jnp.mean(r, axis=-1, keepdims=True)
    c = r - mu
    var = jnp.mean(c * c, axis=-1, keepdims=True)
    return c * lax.rsqrt(var + LN_EPS) * g + b


def _proj_glu_kernel(x_ref, wa_ref, wb_ref, o_ref):
    x = x_ref[...]
    o_ref[...] = _dot(x, wa_ref[...]) * jax.nn.sigmoid(_dot(x, wb_ref[...]))


def _proj_qkv_kernel(x_ref, w_ref, o_ref):
    o_ref[...] = _dot(x_ref[...], w_ref[...]).astype(o_ref.dtype)


def _proj_gate_kernel(x_ref, w_ref, b_ref, o_ref):
    o_ref[...] = jax.nn.sigmoid(_dot(x_ref[...], w_ref[...]) + b_ref[...])


def _input_projection(xb, w_in_b, b_gate):
    t = xb.shape[0]
    tm = 1024
    x_spec = pl.BlockSpec((tm, D_MODEL), lambda i, j: (i, 0))

    def w_spec(col0, tn):
        return pl.BlockSpec((D_MODEL, tn), lambda i, j: (0, j + col0 // tn))

    def o_spec(tn):
        return pl.BlockSpec((tm, tn), lambda i, j: (i, j))

    tn = 512
    glu = pl.pallas_call(
        _proj_glu_kernel, name="proj_glu",
        grid=(t // tm, CONV_CH // tn),
        in_specs=[x_spec, w_spec(0, tn), w_spec(CONV_CH, tn)],
        out_specs=o_spec(tn),
        out_shape=jax.ShapeDtypeStruct((t, CONV_CH), jnp.float32),
        compiler_params=_params("parallel", "parallel"),
    )(xb, w_in_b, w_in_b)
    tn = 1024
    qkv = pl.pallas_call(
        _proj_qkv_kernel, name="proj_qkv",
        grid=(t // tm, COL_QKV // tn),
        in_specs=[x_spec, w_spec(COL_CONV, tn)],
        out_specs=o_spec(tn),
        out_shape=jax.ShapeDtypeStruct((t, COL_QKV), jnp.bfloat16),
        compiler_params=_params("parallel", "parallel"),
    )(xb, w_in_b)
    gates = pl.pallas_call(
        _proj_gate_kernel, name="proj_gate",
        grid=(t // tm, COL_GATE // tn),
        in_specs=[x_spec, w_spec(COL_CONV + COL_QKV, tn), pl.BlockSpec((1, tn), lambda i, j: (0, j))],
        out_specs=o_spec(tn),
        out_shape=jax.ShapeDtypeStruct((t, COL_GATE), jnp.float32),
        compiler_params=_params("parallel", "parallel"),
    )(xb, w_in_b, b_gate.reshape(1, COL_GATE))
    return glu, qkv, gates


CONV_HALO = 32


def _conv_kernel(prev_ref, cur_ref, dww_ref, dwb_ref, lng_ref, lnb_ref, wpw_ref, o_ref, ext_ref, *, ts):
    i = pl.program_id(1)
    halo = prev_ref[0]
    ext_ref[pl.ds(0, CONV_HALO), :] = jnp.where(i > 0, halo, jnp.zeros_like(halo))
    ext_ref[pl.ds(CONV_HALO, ts), :] = cur_ref[0]
    off0 = CONV_HALO - (CONV_WIDTH - 1)
    acc = jnp.zeros((ts, CONV_CH), jnp.float32)
    for j in range(CONV_WIDTH):
        acc = acc + ext_ref[pl.ds(off0 + j, ts), :] * dww_ref[pl.ds(j, 1), :]
    acc = acc + dwb_ref[...]
    y = _layer_norm(acc, lng_ref[...], lnb_ref[...])
    y = y * jax.nn.sigmoid(y)
    o_ref[...] = _dot(y.astype(jnp.bfloat16), wpw_ref[...])


def _conformer_conv(glu, dw_w, dw_b, ln_g, ln_b, w_pw_b, batch, seq):
    ts = 256
    h3 = glu.reshape(batch, seq, CONV_CH)
    nblk = ts // CONV_HALO
    vec = lambda: pl.BlockSpec((1, CONV_CH), lambda b, i: (0, 0))
    return pl.pallas_call(
        functools.partial(_conv_kernel, ts=ts), name="conv_module",
        grid=(batch, seq // ts),
        in_specs=[
            pl.BlockSpec((1, CONV_HALO, CONV_CH), lambda b, i: (b, jnp.maximum(i * nblk - 1, 0), 0)),
            pl.BlockSpec((1, ts, CONV_CH), lambda b, i: (b, i, 0)),
            pl.BlockSpec((CONV_WIDTH, CONV_CH), lambda b, i: (0, 0)),
            vec(), vec(), vec(),
            pl.BlockSpec((CONV_CH, D_MODEL), lambda b, i: (0, 0)),
        ],
        out_specs=pl.BlockSpec((ts, D_MODEL), lambda b, i: (b * (seq // ts) + i, 0)),
        out_shape=jax.ShapeDtypeStruct((batch * seq, D_MODEL), jnp.float32),
        scratch_shapes=[pltpu.VMEM((CONV_HALO + ts, CONV_CH), jnp.float32)],
        compiler_params=_params("parallel", "parallel"),
    )(h3, h3, dw_w.reshape(CONV_WIDTH, CONV_CH), dw_b.reshape(1, CONV_CH),
      ln_g.reshape(1, CONV_CH), ln_b.reshape(1, CONV_CH), w_pw_b)


def _attn_kernel(slopes_ref, q_ref, k_ref, v_ref, lq1_ref, lk1_ref, lq2_ref, lk2_ref, g_ref, o_ref, *, tq):
    h = pl.program_id(1)
    qi = pl.program_id(2)
    slope = slopes_ref[h]
    q = q_ref[0] * (ATTN_HEAD_DIM ** -0.5)
    lane = lax.broadcasted_iota(jnp.int32, q.shape, 1)
    zero = jnp.zeros_like(q)
    q1 = jnp.where(lane < ATTN_HEAD_DIM, q, zero)
    q2 = jnp.where(lane >= ATTN_HEAD_DIM, q, zero)
    causal = (lax.broadcasted_iota(jnp.int32, (tq, tq), 0)
              >= lax.broadcasted_iota(jnp.int32, (tq, tq), 1))
    kcol = lax.broadcasted_iota(jnp.int32, (1, tq), 1)

    def block(kb, carry, diagonal):
        m1, l1, a1, m2, l2, a2 = carry
        k = k_ref[0, pl.ds(kb * tq, tq), :]
        v = v_ref[0, pl.ds(kb * tq, tq), :]
        bias = slope * ((kb - qi) * tq + kcol).astype(jnp.float32)

        def one(qm, m, l, a):
            s = _dot_nt(qm, k) + bias
            if diagonal:
                s = jnp.where(causal, s, NEG_INF)
            m_new = jnp.maximum(m, jnp.max(s, axis=-1, keepdims=True))
            p = jnp.exp(s - m_new)
            corr = jnp.exp(m - m_new)
            l_new = corr * l + jnp.sum(p, axis=-1, keepdims=True)
            a_new = corr * a + _dot(p.astype(v.dtype), v)
            return m_new, l_new, a_new

        m1, l1, a1 = one(q1, m1, l1, a1)
        m2, l2, a2 = one(q2, m2, l2, a2)
        return m1, l1, a1, m2, l2, a2

    col = lambda val: jnp.full((tq, 1), val, jnp.float32)
    acc0 = jnp.zeros((tq, ATTN_V_DIM), jnp.float32)
    init = (col(NEG_INF), col(0.0), acc0, col(NEG_INF), col(0.0), acc0)
    carry = lax.fori_loop(0, qi, functools.partial(block, diagonal=False), init)
    m1, l1, a1, m2, l2, a2 = block(qi, carry, diagonal=True)

    lam = (jnp.exp(jnp.sum(lq1_ref[...] * lk1_ref[...], axis=-1, keepdims=True))
           - jnp.exp(jnp.sum(lq2_ref[...] * lk2_ref[...], axis=-1, keepdims=True)) + LAM_INIT)
    o = a1 / l1 - lam * (a2 / l2)
    ms = jnp.mean(o * o, axis=-1, keepdims=True)
    o = o * lax.rsqrt(ms + LN_EPS) * g_ref[...] * (1.0 - LAM_INIT)
    o_ref[0] = o.astype(o_ref.dtype)


def _diff_attention(qkv, lq1, lk1, lq2, lk2, subln_g, batch, seq):
    tq = 512
    qkv3 = qkv.reshape(batch, seq, COL_QKV)
    slopes = jnp.asarray(2.0 ** (-8.0 * np.arange(1, ATTN_HEADS + 1) / ATTN_HEADS), jnp.float32)
    hd = lambda: pl.BlockSpec((1, ATTN_HEAD_DIM), lambda b, h, i: (0, 0))
    return pl.pallas_call(
        functools.partial(_attn_kernel, tq=tq), name="diff_attention",
        grid=(batch, ATTN_HEADS, seq // tq),
        in_specs=[
            pl.BlockSpec(memory_space=pltpu.SMEM),
            pl.BlockSpec((1, tq, ATTN_V_DIM), lambda b, h, i: (b, i, h)),
            pl.BlockSpec((1, seq, ATTN_V_DIM), lambda b, h, i: (b, 0, ATTN_HEADS + h)),
            pl.BlockSpec((1, seq, ATTN_V_DIM), lambda b, h, i: (b, 0, 2 * ATTN_HEADS + h)),
            hd(), hd(), hd(), hd(),
            pl.BlockSpec((1, ATTN_V_DIM), lambda b, h, i: (0, 0)),
        ],
        out_specs=pl.BlockSpec((1, tq, ATTN_V_DIM), lambda b, h, i: (b, i, h)),
        out_shape=jax.ShapeDtypeStruct((batch, seq, ATTN_HEADS * ATTN_V_DIM), jnp.bfloat16),
        compiler_params=_params("parallel", "parallel", "arbitrary"),
    )(slopes, qkv3, qkv3, qkv3, lq1.reshape(1, -1), lk1.reshape(1, -1), lq2.reshape(1, -1),
      lk2.reshape(1, -1), subln_g.reshape(1, -1))


def _mix_kernel(gate_ref, yc_ref, on_ref, wo_ref, wout_ref, x_ref, g_ref, b_ref, x1_ref, x1b_ref):
    y_attn = _dot(on_ref[...], wo_ref[...])
    mixed = gate_ref[:, :D_MODEL] * yc_ref[...] + gate_ref[:, D_MODEL:] * y_attn
    z = _dot(mixed.astype(jnp.bfloat16), wout_ref[...])
    x1 = _layer_norm(ALPHA * x_ref[...] + z, g_ref[...], b_ref[...])
    x1_ref[...] = x1
    x1b_ref[...] = x1.astype(jnp.bfloat16)


def _mix(gates, y_conv, o_norm, w_o_b, w_out_b, x2d, ln_g, ln_b):
    t = x2d.shape[0]
    tm = 256
    row = lambda n: pl.BlockSpec((tm, n), lambda i: (i, 0))
    full = lambda a, b: pl.BlockSpec((a, b), lambda i: (0, 0))
    return pl.pallas_call(
        _mix_kernel, name="mix_deepnorm",
        grid=(t // tm,),
        in_specs=[row(COL_GATE), row(D_MODEL), row(ATTN_HEADS * ATTN_V_DIM),
                  full(ATTN_HEADS * ATTN_V_DIM, D_MODEL), full(D_MODEL, D_MODEL), row(D_MODEL),
                  full(1, D_MODEL), full(1, D_MODEL)],
        out_specs=[row(D_MODEL), row(D_MODEL)],
        out_shape=[jax.ShapeDtypeStruct((t, D_MODEL), jnp.float32),
                   jax.ShapeDtypeStruct((t, D_MODEL), jnp.bfloat16)],
        compiler_params=_params("parallel"),
    )(gates, y_conv, o_norm, w_o_b, w_out_b, x2d, ln_g.reshape(1, -1), ln_b.reshape(1, -1))


def _peer_prep_kernel(x_ref, wq_ref, keys_ref, cnt_ref, e1_ref, rank2_ref, e2_ref, top_ref, s1_ref, rank1_ref,
                      *, tm):
    q = _dot(x_ref[...], wq_ref[...]).astype(jnp.bfloat16)
    for c in range(2):
        for h in range(PEER_HEADS):
            idx = c * PEER_HEADS + h
            s = _dot_nt(keys_ref[idx], q[:, idx * PEER_HALF:(idx + 1) * PEER_HALF])
            work = s
            rank = jnp.full(s.shape, float(PEER_TOPK), jnp.float32)
            for r in range(PEER_TOPK):
                mx = jnp.max(work, axis=0, keepdims=True)
                top_ref[c, r, pl.ds(h, 1), :] = mx
                hit = work == mx
                rank = jnp.where(hit, float(r), rank)
                work = jnp.where(hit, NEG_INF, work)
            if c == 0:
                s1_ref[h] = s
                rank1_ref[h] = rank
            else:
                rank2_ref[h] = rank.astype(rank2_ref.dtype)
                e2_ref[h] = jnp.exp(s - top_ref[1, 0, pl.ds(h, 1), :]).astype(e2_ref.dtype)
    cands = [top_ref[0, i] + top_ref[1, j] for (i, j) in CAND_PAIRS]
    work = list(cands)
    remaining = jnp.full((PEER_HEADS, tm), float(PEER_TOPK), jnp.float32)
    tau = jnp.full((PEER_HEADS, tm), NEG_INF, jnp.float32)
    for _ in range(PEER_TOPK):
        mx = functools.reduce(jnp.maximum, work)
        hit = [w == mx for w in work]
        cnt = functools.reduce(jnp.add, [hm.astype(jnp.float32) for hm in hit])
        newly = jnp.logical_and(remaining > 0.0, cnt >= remaining)
        tau = jnp.where(newly, mx, tau)
        remaining = remaining - cnt
        work = [jnp.where(hm, NEG_INF, w) for hm, w in zip(hit, work)]
    m1 = top_ref[0, 0]
    m2 = top_ref[1, 0]
    ex1 = [jnp.exp(top_ref[0, i] - m1) for i in range(PEER_TOPK)]
    ex2 = [jnp.exp(top_ref[1, j] - m2) for j in range(PEER_TOPK)]
    z = jnp.zeros((PEER_HEADS, tm), jnp.float32)
    sel_count = [jnp.zeros((PEER_HEADS, tm), jnp.float32) for _ in range(PEER_TOPK)]
    for cand, (i, j) in zip(cands, CAND_PAIRS):
        sel = cand >= tau
        z = z + jnp.where(sel, ex1[i] * ex2[j], 0.0)
        sel_count[i] = sel_count[i] + sel.astype(jnp.float32)
    for h in range(PEER_HEADS):
        rank1 = rank1_ref[h]
        cnt = jnp.zeros(rank1.shape, jnp.float32)
        for i in range(PEER_TOPK):
            cnt = jnp.where(rank1 == float(i), sel_count[i][h:h + 1, :], cnt)
        cnt_ref[h] = cnt
        e1_ref[h] = jnp.exp(s1_ref[h] - m1[h:h + 1, :]) / z[h:h + 1, :]


def _peer_prep(x1b, wq_b, keys_b):
    t = x1b.shape[0]
    tm = 256
    nq = 2 * PEER_HEADS * PEER_HALF
    big = lambda: pl.BlockSpec((PEER_HEADS, PEER_N_KEYS, tm), lambda i: (0, 0, i))
    f32_shape = jax.ShapeDtypeStruct((PEER_HEADS, PEER_N_KEYS, t), jnp.float32)
    bf16_shape = jax.ShapeDtypeStruct((PEER_HEADS, PEER_N_KEYS, t), jnp.bfloat16)
    return pl.pallas_call(
        functools.partial(_peer_prep_kernel, tm=tm), name="peer_prep",
        grid=(t // tm,),
        in_specs=[pl.BlockSpec((tm, D_MODEL), lambda i: (i, 0)),
                  pl.BlockSpec((D_MODEL, nq), lambda i: (0, 0)),
                  pl.BlockSpec((2 * PEER_HEADS, PEER_N_KEYS, PEER_HALF), lambda i: (0, 0, 0))],
        out_specs=[big(), big(), big(), big()],
        out_shape=[f32_shape, f32_shape, bf16_shape, bf16_shape],
        scratch_shapes=[pltpu.VMEM((2, PEER_TOPK, PEER_HEADS, tm), jnp.float32),
                        pltpu.VMEM((PEER_HEADS, PEER_N_KEYS, tm), jnp.float32),
                        pltpu.VMEM((PEER_HEADS, PEER_N_KEYS, tm), jnp.float32)],
        compiler_params=_params("parallel"),
    )(x1b, wq_b, keys_b)


def _gelu_tanh(x):
    return 0.5 * x * (1.0 + jnp.tanh(math.sqrt(2.0 / math.pi) * (x + 0.044715 * (x * x * x))))


BF16_ROWS = 16


def _peer_dense_kernel(xb_ref, u_ref, v_ref, cnt_ref, e1_ref, rank2_ref, e2_ref, o_ref, acc_ref, wd_ref, *, tm, te):
    j = pl.program_id(1)
    n_tiles = pl.num_programs(1) - 1
    jc = jnp.minimum(j, n_tiles - 1)

    @pl.when(j == 0)
    def _():
        acc_ref[...] = jnp.zeros_like(acc_ref)
        wd_ref[...] = jnp.zeros_like(wd_ref)

    acc_ref[...] += _dot_tn(wd_ref[...], v_ref[...])
    hid = _dot_nt(u_ref[...], xb_ref[...])
    bf = jnp.bfloat16
    groups = PEER_N_KEYS // BF16_ROWS
    for k in range(te // PEER_N_KEYS):
        a = jc * (te // PEER_N_KEYS) + k
        gate = [jnp.zeros((BF16_ROWS, tm), bf) for _ in range(groups)]
        for h in range(PEER_HEADS):
            cnt = jnp.broadcast_to(cnt_ref[h, pl.ds(a, 1), :], (BF16_ROWS, tm)).astype(bf)
            e1 = jnp.broadcast_to(e1_ref[h, pl.ds(a, 1), :], (BF16_ROWS, tm)).astype(bf)
            for g in range(groups):
                rows = pl.ds(g * BF16_ROWS, BF16_ROWS)
                sel = rank2_ref[h, rows, :] < cnt
                gate[g] = gate[g] + jnp.where(sel, e1 * e2_ref[h, rows, :], jnp.zeros((), bf))
        for g in range(groups):
            r0 = k * PEER_N_KEYS + g * BF16_ROWS
            act = _gelu_tanh(hid[r0:r0 + BF16_ROWS, :]).astype(bf)
            wd_ref[pl.ds(r0, BF16_ROWS), :] = gate[g] * act

    @pl.when(j == n_tiles)
    def _():
        o_ref[...] = acc_ref[...]


def _peer_dense(x1b, u_b, v_b, cnt, e1, rank2, e2):
    t = x1b.shape[0]
    n_exp = u_b.shape[0]
    tm, te = 512, 512
    n_tiles = n_exp // te
    big = lambda: pl.BlockSpec((PEER_HEADS, PEER_N_KEYS, tm), lambda i, j: (0, 0, i))
    row = lambda: pl.BlockSpec((tm, D_MODEL), lambda i, j: (i, 0))
    return pl.pallas_call(
        functools.partial(_peer_dense_kernel, tm=tm, te=te), name="peer_dense",
        grid=(t // tm, n_tiles + 1),
        in_specs=[row(),
                  pl.BlockSpec((te, D_MODEL), lambda i, j: (jnp.minimum(j, n_tiles - 1), 0)),
                  pl.BlockSpec((te, D_MODEL), lambda i, j: (jnp.maximum(j - 1, 0), 0)),
                  big(), big(), big(), big()],
        out_specs=row(),
        out_shape=jax.ShapeDtypeStruct((t, D_MODEL), jnp.float32),
        scratch_shapes=[pltpu.VMEM((tm, D_MODEL), jnp.float32), pltpu.VMEM((te, tm), jnp.bfloat16)],
        compiler_params=_params("parallel", "arbitrary"),
    )(x1b, u_b, v_b, cnt, e1, rank2, e2)


def _ple_kernel(x1_ref, po_ref, g_ref, b_ref, wg_ref, bg_ref, p_ref, wp_ref, o_ref, x2_ref, x2b_ref, *, tn):
    j = pl.program_id(1)

    @pl.when(j == 0)
    def _():
        x2 = _layer_norm(ALPHA * x1_ref[...] + po_ref[...], g_ref[...], b_ref[...])
        x2_ref[...] = x2
        x2b_ref[...] = x2.astype(jnp.bfloat16)

    gate = jax.nn.sigmoid(_dot(x2b_ref[...], wg_ref[...]) + bg_ref[...])
    col = pl.multiple_of(j * tn, tn)
    o_ref[...] = x2_ref[:, pl.ds(col, tn)] + gate * _dot(p_ref[...], wp_ref[...])


def _ple(x1, peer_out, ln_g, ln_b, w_gate_b, b_gate, p_b, w_proj_b):
    t = x1.shape[0]
    tm, tn = 512, 512
    row = lambda: pl.BlockSpec((tm, D_MODEL), lambda i, j: (i, 0))
    vec = lambda: pl.BlockSpec((1, D_MODEL), lambda i, j: (0, 0))
    return pl.pallas_call(
        functools.partial(_ple_kernel, tn=tn), name="ple",
        grid=(t // tm, D_MODEL // tn),
        in_specs=[row(), row(), vec(), vec(),
                  pl.BlockSpec((D_MODEL, tn), lambda i, j: (0, j)),
                  pl.BlockSpec((1, tn), lambda i, j: (0, j)),
                  pl.BlockSpec((tm, P_DIM), lambda i, j: (i, 0)),
                  pl.BlockSpec((P_DIM, tn), lambda i, j: (0, j))],
        out_specs=pl.BlockSpec((tm, tn), lambda i, j: (i, j)),
        out_shape=jax.ShapeDtypeStruct((t, D_MODEL), jnp.float32),
        scratch_shapes=[pltpu.VMEM((tm, D_MODEL), jnp.float32), pltpu.VMEM((tm, D_MODEL), jnp.bfloat16)],
        compiler_params=_params("parallel", "arbitrary"),
    )(x1, peer_out, ln_g.reshape(1, -1), ln_b.reshape(1, -1), w_gate_b, b_gate.reshape(1, -1), p_b, w_proj_b)


def kernel(x, p, w_in, b_gate, conv_dw_w, conv_dw_b, conv_ln_g, conv_ln_b, conv_w_out, attn_lambda_q1, attn_lambda_k1, attn_lambda_q2, attn_lambda_k2, attn_subln_g, attn_w_o, w_out, ln1_g, ln1_b, peer_w_q, peer_sub_keys, peer_u, peer_v, ln2_g, ln2_b, ple_w_proj, ple_w_gate, ple_b_gate):
    batch, seq, d = x.shape
    t = batch * seq
    bf = jnp.bfloat16
    assert w_in.shape[0] == DEPTH == 1 and d == D_MODEL
    x2d = x.reshape(t, d)

    glu, qkv, gates = _input_projection(x2d.astype(bf), w_in[0].astype(bf), b_gate[0])
    y_conv = _conformer_conv(glu, conv_dw_w[0], conv_dw_b[0], conv_ln_g[0], conv_ln_b[0],
                             conv_w_out[0].astype(bf), batch, seq)
    o_norm = _diff_attention(qkv, attn_lambda_q1[0], attn_lambda_k1[0], attn_lambda_q2[0],
                             attn_lambda_k2[0], attn_subln_g[0], batch, seq)
    x1, x1b = _mix(gates, y_conv, o_norm.reshape(t, -1), attn_w_o[0].astype(bf), w_out[0].astype(bf),
                   x2d, ln1_g[0], ln1_b[0])

    wq = peer_w_q[0].reshape(d, PEER_HEADS, 2, PEER_HALF).transpose(0, 2, 1, 3).reshape(d, -1).astype(bf)
    keys = peer_sub_keys[0].transpose(1, 0, 2, 3).reshape(2 * PEER_HEADS, PEER_N_KEYS, PEER_HALF).astype(bf)
    cnt, e1, rank2, e2 = _peer_prep(x1b, wq, keys)
    peer_out = _peer_dense(x1b, peer_u[0].astype(bf), peer_v[0].astype(bf), cnt, e1, rank2, e2)

    out = _ple(x1, peer_out, ln2_g[0], ln2_b[0], ple_w_gate[0].astype(bf), ple_b_gate[0],
               p[0].reshape(t, P_DIM).astype(bf), ple_w_proj[0].astype(bf))
    return out.reshape(batch, seq, d)
```

```python
import functools
import math

import jax
import jax.numpy as jnp
import numpy as np
from jax import lax
from jax.experimental import pallas as pl
from jax.experimental.pallas import tpu as pltpu

D_MODEL = 2048
CONV_CH = 1024
CONV_WIDTH = 31
ATTN_HEADS = 8
ATTN_HEAD_DIM = 64
ATTN_V_DIM = 2 * ATTN_HEAD_DIM
PEER_HEADS = 8
PEER_N_KEYS = 128
PEER_HALF = 128
PEER_TOPK = 16
P_DIM = 256
LN_EPS = 1e-5
DEPTH = 1
ALPHA = (2.0 * DEPTH) ** 0.25
LAM_INIT = 0.8 - 0.6 * math.exp(-0.3 * 0)

COL_CONV = 2 * CONV_CH
COL_QKV = 3 * ATTN_HEADS * ATTN_V_DIM
COL_GATE = 2 * D_MODEL

VMEM_LIMIT_BYTES = 56 * 1024 * 1024
NEG_INF = float("-inf")

CAND_PAIRS = tuple((i, j) for i in range(PEER_TOPK) for j in range(PEER_TOPK)
                   if (i + 1) * (j + 1) <= PEER_TOPK)


def _params(*sem, **kw):
    return pltpu.CompilerParams(dimension_semantics=sem, vmem_limit_bytes=VMEM_LIMIT_BYTES, **kw)


def _dot(a, b):
    return jnp.dot(a, b, preferred_element_type=jnp.float32)


def _dot_nt(a, b):
    return lax.dot_general(a, b, (((1,), (1,)), ((), ())), preferred_element_type=jnp.float32)


def _dot_tn(a, b):
    return lax.dot_general(a, b, (((0,), (0,)), ((), ())), preferred_element_type=jnp.float32)


def _layer_norm(r, g, b):
    mu = jnp.mean(r, axis=-1, keepdims=True)
    c = r - mu
    var = jnp.mean(c * c, axis=-1, keepdims=True)
    return c * lax.rsqrt(var + LN_EPS) * g + b


def _proj_glu_kernel(x_ref, wa_ref, wb_ref, o_ref):
    x = x_ref[...]
    o_ref[...] = _dot(x, wa_ref[...]) * jax.nn.sigmoid(_dot(x, wb_ref[...]))


def _proj_qkv_kernel(x_ref, w_ref, o_ref):
    o_ref[...] = _dot(x_ref[...], w_ref[...]).astype(o_ref.dtype)


def _proj_gate_kernel(x_ref, w_ref, b_ref, o_ref):
    o_ref[...] = jax.nn.sigmoid(_dot(x_ref[...], w_ref[...]) + b_ref[...])


def _input_projection(xb, w_in_b, b_gate):
    t = xb.shape[0]
    tm = 1024
    x_spec = pl.BlockSpec((tm, D_MODEL), lambda i, j: (i, 0))

    def w_spec(col0, tn):
        return pl.BlockSpec((D_MODEL, tn), lambda i, j: (0, j + col0 // tn))

    def o_spec(tn):
        return pl.BlockSpec((tm, tn), lambda i, j: (i, j))

    tn = 512
    glu = pl.pallas_call(
        _proj_glu_kernel, name="proj_glu",
        grid=(t // tm, CONV_CH // tn),
        in_specs=[x_spec, w_spec(0, tn), w_spec(CONV_CH, tn)],
        out_specs=o_spec(tn),
        out_shape=jax.ShapeDtypeStruct((t, CONV_CH), jnp.float32),
        compiler_params=_params("parallel", "parallel"),
    )(xb, w_in_b, w_in_b)
    tn = 1024
    qkv = pl.pallas_call(
        _proj_qkv_kernel, name="proj_qkv",
        grid=(t // tm, COL_QKV // tn),
        in_specs=[x_spec, w_spec(COL_CONV, tn)],
        out_specs=o_spec(tn),
        out_shape=jax.ShapeDtypeStruct((t, COL_QKV), jnp.bfloat16),
        compiler_params=_params("parallel", "parallel"),
    )(xb, w_in_b)
    gates = pl.pallas_call(
        _proj_gate_kernel, name="proj_gate",
        grid=(t // tm, COL_GATE // tn),
        in_specs=[x_spec, w_spec(COL_CONV + COL_QKV, tn), pl.BlockSpec((1, tn), lambda i, j: (0, j))],
        out_specs=o_spec(tn),
        out_shape=jax.ShapeDtypeStruct((t, COL_GATE), jnp.float32),
        compiler_params=_params("parallel", "parallel"),
    )(xb, w_in_b, b_gate.reshape(1, COL_GATE))
    return glu, qkv, gates


SUBLANES = 8
CONV_HALO = 32


def _conv_kernel(prev_ref, cur_ref, dww_ref, dwb_ref, lng_ref, lnb_ref, wpw_ref, o_ref, ext_ref, *, ts):
    i = pl.program_id(1)
    halo = prev_ref[0]
    ext_ref[pl.ds(0, CONV_HALO), :] = jnp.where(i > 0, halo, jnp.zeros_like(halo))
    ext_ref[pl.ds(CONV_HALO, ts), :] = cur_ref[0]
    off0 = CONV_HALO - (CONV_WIDTH - 1)
    acc = jnp.zeros((ts, CONV_CH), jnp.float32)
    ext = ext_ref[...]
    n_ext = CONV_HALO + ts
    for r in range(SUBLANES):
        offsets = [o for o in range(r, CONV_HALO + 1, SUBLANES) if 0 <= o - off0 < CONV_WIDTH]
        shifted = ext if r == 0 else pltpu.roll(ext, n_ext - r, axis=0)
        for o in offsets:
            acc = acc + shifted[o - r:o - r + ts, :] * dww_ref[pl.ds(o - off0, 1), :]
    acc = acc + dwb_ref[...]
    y = _layer_norm(acc, lng_ref[...], lnb_ref[...])
    y = y * jax.nn.sigmoid(y)
    o_ref[...] = _dot(y.astype(jnp.bfloat16), wpw_ref[...])


def _conformer_conv(glu, dw_w, dw_b, ln_g, ln_b, w_pw_b, batch, seq):
    ts = 256
    h3 = glu.reshape(batch, seq, CONV_CH)
    nblk = ts // CONV_HALO
    vec = lambda: pl.BlockSpec((1, CONV_CH), lambda b, i: (0, 0))
    return pl.pallas_call(
        functools.partial(_conv_kernel, ts=ts), name="conv_module",
        grid=(batch, seq // ts),
        in_specs=[
            pl.BlockSpec((1, CONV_HALO, CONV_CH), lambda b, i: (b, jnp.maximum(i * nblk - 1, 0), 0)),
            pl.BlockSpec((1, ts, CONV_CH), lambda b, i: (b, i, 0)),
            pl.BlockSpec((CONV_WIDTH, CONV_CH), lambda b, i: (0, 0)),
            vec(), vec(), vec(),
            pl.BlockSpec((CONV_CH, D_MODEL), lambda b, i: (0, 0)),
        ],
        out_specs=pl.BlockSpec((ts, D_MODEL), lambda b, i: (b * (seq // ts) + i, 0)),
        out_shape=jax.ShapeDtypeStruct((batch * seq, D_MODEL), jnp.float32),
        scratch_shapes=[pltpu.VMEM((CONV_HALO + ts, CONV_CH), jnp.float32)],
        compiler_params=_params("parallel", "parallel"),
    )(h3, h3, dw_w.reshape(CONV_WIDTH, CONV_CH), dw_b.reshape(1, CONV_CH),
      ln_g.reshape(1, CONV_CH), ln_b.reshape(1, CONV_CH), w_pw_b)


def _attn_kernel(slopes_ref, q_ref, k_ref, v_ref, lq1_ref, lk1_ref, lq2_ref, lk2_ref, g_ref, o_ref, *, tq):
    h = pl.program_id(1)
    qi = pl.program_id(2)
    slope = slopes_ref[h]
    q = q_ref[0] * (ATTN_HEAD_DIM ** -0.5)
    lane = lax.broadcasted_iota(jnp.int32, q.shape, 1)
    zero = jnp.zeros_like(q)
    q1 = jnp.where(lane < ATTN_HEAD_DIM, q, zero)
    q2 = jnp.where(lane >= ATTN_HEAD_DIM, q, zero)
    causal = (lax.broadcasted_iota(jnp.int32, (tq, tq), 0)
              >= lax.broadcasted_iota(jnp.int32, (tq, tq), 1))
    kcol = lax.broadcasted_iota(jnp.int32, (1, tq), 1)

    def block(kb, carry, diagonal):
        m1, l1, a1, m2, l2, a2 = carry
        k = k_ref[0, pl.ds(kb * tq, tq), :]
        v = v_ref[0, pl.ds(kb * tq, tq), :]
        bias = slope * ((kb - qi) * tq + kcol).astype(jnp.float32)

        def one(qm, m, l, a):
            s = _dot_nt(qm, k) + bias
            if diagonal:
                s = jnp.where(causal, s, NEG_INF)
            m_new = jnp.maximum(m, jnp.max(s, axis=-1, keepdims=True))
            p = jnp.exp(s - m_new)
            corr = jnp.exp(m - m_new)
            l_new = corr * l + jnp.sum(p, axis=-1, keepdims=True)
            a_new = corr * a + _dot(p.astype(v.dtype), v)
            return m_new, l_new, a_new

        m1, l1, a1 = one(q1, m1, l1, a1)
        m2, l2, a2 = one(q2, m2, l2, a2)
        return m1, l1, a1, m2, l2, a2

    col = lambda val: jnp.full((tq, 1), val, jnp.float32)
    acc0 = jnp.zeros((tq, ATTN_V_DIM), jnp.float32)
    init = (col(NEG_INF), col(0.0), acc0, col(NEG_INF), col(0.0), acc0)
    carry = lax.fori_loop(0, qi, functools.partial(block, diagonal=False), init)
    m1, l1, a1, m2, l2, a2 = block(qi, carry, diagonal=True)

    lam = (jnp.exp(jnp.sum(lq1_ref[...] * lk1_ref[...], axis=-1, keepdims=True))
           - jnp.exp(jnp.sum(lq2_ref[...] * lk2_ref[...], axis=-1, keepdims=True)) + LAM_INIT)
    o = a1 / l1 - lam * (a2 / l2)
    ms = jnp.mean(o * o, axis=-1, keepdims=True)
    o = o * lax.rsqrt(ms + LN_EPS) * g_ref[...] * (1.0 - LAM_INIT)
    o_ref[0] = o.astype(o_ref.dtype)


def _diff_attention(qkv, lq1, lk1, lq2, lk2, subln_g, batch, seq):
    tq = 512
    qkv3 = qkv.reshape(batch, seq, COL_QKV)
    slopes = jnp.asarray(2.0 ** (-8.0 * np.arange(1, ATTN_HEADS + 1) / ATTN_HEADS), jnp.float32)
    hd = lambda: pl.BlockSpec((1, ATTN_HEAD_DIM), lambda b, h, i: (0, 0))
    return pl.pallas_call(
        functools.partial(_attn_kernel, tq=tq), name="diff_attention",
        grid=(batch, ATTN_HEADS, seq // tq),
        in_specs=[
            pl.BlockSpec(memory_space=pltpu.SMEM),
            pl.BlockSpec((1, tq, ATTN_V_DIM), lambda b, h, i: (b, i, h)),
            pl.BlockSpec((1, seq, ATTN_V_DIM), lambda b, h, i: (b, 0, ATTN_HEADS + h)),
            pl.BlockSpec((1, seq, ATTN_V_DIM), lambda b, h, i: (b, 0, 2 * ATTN_HEADS + h)),
            hd(), hd(), hd(), hd(),
            pl.BlockSpec((1, ATTN_V_DIM), lambda b, h, i: (0, 0)),
        ],
        out_specs=pl.BlockSpec((1, tq, ATTN_V_DIM), lambda b, h, i: (b, i, h)),
        out_shape=jax.ShapeDtypeStruct((batch, seq, ATTN_HEADS * ATTN_V_DIM), jnp.bfloat16),
        compiler_params=_params("parallel", "parallel", "arbitrary"),
    )(slopes, qkv3, qkv3, qkv3, lq1.reshape(1, -1), lk1.reshape(1, -1), lq2.reshape(1, -1),
      lk2.reshape(1, -1), subln_g.reshape(1, -1))


def _mix_kernel(gate_ref, yc_ref, on_ref, wo_ref, wout_ref, x_ref, g_ref, b_ref, x1_ref, x1b_ref):
    y_attn = _dot(on_ref[...], wo_ref[...])
    mixed = gate_ref[:, :D_MODEL] * yc_ref[...] + gate_ref[:, D_MODEL:] * y_attn
    z = _dot(mixed.astype(jnp.bfloat16), wout_ref[...])
    x1 = _layer_norm(ALPHA * x_ref[...] + z, g_ref[...], b_ref[...])
    x1_ref[...] = x1
    x1b_ref[...] = x1.astype(jnp.bfloat16)


def _mix(gates, y_conv, o_norm, w_o_b, w_out_b, x2d, ln_g, ln_b):
    t = x2d.shape[0]
    tm = 256
    row = lambda n: pl.BlockSpec((tm, n), lambda i: (i, 0))
    full = lambda a, b: pl.BlockSpec((a, b), lambda i: (0, 0))
    return pl.pallas_call(
        _mix_kernel, name="mix_deepnorm",
        grid=(t // tm,),
        in_specs=[row(COL_GATE), row(D_MODEL), row(ATTN_HEADS * ATTN_V_DIM),
                  full(ATTN_HEADS * ATTN_V_DIM, D_MODEL), full(D_MODEL, D_MODEL), row(D_MODEL),
                  full(1, D_MODEL), full(1, D_MODEL)],
        out_specs=[row(D_MODEL), row(D_MODEL)],
        out_shape=[jax.ShapeDtypeStruct((t, D_MODEL), jnp.float32),
                   jax.ShapeDtypeStruct((t, D_MODEL), jnp.bfloat16)],
        compiler_params=_params("parallel"),
    )(gates, y_conv, o_norm, w_o_b, w_out_b, x2d, ln_g.reshape(1, -1), ln_b.reshape(1, -1))


def _peer_prep_kernel(x_ref, wq_ref, keys_ref, cnt_ref, e1_ref, rank2_ref, e2_ref, top_ref, s1_ref, *, tm):
    q = _dot(x_ref[...], wq_ref[...]).astype(jnp.bfloat16)
    for c in range(2):
        for h in range(PEER_HEADS):
            idx = c * PEER_HEADS + h
            s = _dot_nt(keys_ref[idx], q[:, idx * PEER_HALF:(idx + 1) * PEER_HALF])
            work = s
            rank = jnp.full(s.shape, float(PEER_TOPK), jnp.float32)
            for r in range(PEER_TOPK):
                mx = jnp.max(work, axis=0, keepdims=True)
                top_ref[c, r, pl.ds(h, 1), :] = mx
                hit = work == mx
                if c == 1:
                    rank = jnp.where(hit, float(r), rank)
                work = jnp.where(hit, NEG_INF, work)
            if c == 0:
                s1_ref[h] = s
            else:
                rank2_ref[h] = rank.astype(rank2_ref.dtype)
                e2_ref[h] = jnp.exp(s - top_ref[1, 0, pl.ds(h, 1), :]).astype(e2_ref.dtype)
    cands = [top_ref[0, i] + top_ref[1, j] for (i, j) in CAND_PAIRS]
    work = list(cands)
    remaining = jnp.full((PEER_HEADS, tm), float(PEER_TOPK), jnp.float32)
    tau = jnp.full((PEER_HEADS, tm), NEG_INF, jnp.float32)
    for _ in range(PEER_TOPK):
        mx = functools.reduce(jnp.maximum, work)
        hit = [w == mx for w in work]
        cnt = functools.reduce(jnp.add, [hm.astype(jnp.float32) for hm in hit])
        newly = jnp.logical_and(remaining > 0.0, cnt >= remaining)
        tau = jnp.where(newly, mx, tau)
        remaining = remaining - cnt
        work = [jnp.where(hm, NEG_INF, w) for hm, w in zip(hit, work)]
    m1 = top_ref[0, 0]
    m2 = top_ref[1, 0]
    ex1 = [jnp.exp(top_ref[0, i] - m1) for i in range(PEER_TOPK)]
    ex2 = [jnp.exp(top_ref[1, j] - m2) for j in range(PEER_TOPK)]
    z = jnp.zeros((PEER_HEADS, tm), jnp.float32)
    sel_count = [jnp.zeros((PEER_HEADS, tm), jnp.float32) for _ in range(PEER_TOPK)]
    for cand, (i, j) in zip(cands, CAND_PAIRS):
        sel = cand >= tau
        z = z + jnp.where(sel, ex1[i] * ex2[j], 0.0)
        sel_count[i] = sel_count[i] + sel.astype(jnp.float32)
    for h in range(PEER_HEADS):
        s1 = s1_ref[h]
        cnt = jnp.zeros(s1.shape, jnp.float32)
        for i in range(PEER_TOPK):
            cnt = jnp.where(s1 == top_ref[0, i, pl.ds(h, 1), :], sel_count[i][h:h + 1, :], cnt)
        cnt_ref[h] = cnt
        e1_ref[h] = jnp.exp(s1 - m1[h:h + 1, :]) / z[h:h + 1, :]


def _peer_prep(x1b, wq_b, keys_b):
    t = x1b.shape[0]
    tm = 256
    nq = 2 * PEER_HEADS * PEER_HALF
    big = lambda: pl.BlockSpec((PEER_HEADS, PEER_N_KEYS, tm), lambda i: (0, 0, i))
    f32_shape = jax.ShapeDtypeStruct((PEER_HEADS, PEER_N_KEYS, t), jnp.float32)
    bf16_shape = jax.ShapeDtypeStruct((PEER_HEADS, PEER_N_KEYS, t), jnp.bfloat16)
    return pl.pallas_call(
        functools.partial(_peer_prep_kernel, tm=tm), name="peer_prep",
        grid=(t // tm,),
        in_specs=[pl.BlockSpec((tm, D_MODEL), lambda i: (i, 0)),
                  pl.BlockSpec((D_MODEL, nq), lambda i: (0, 0)),
                  pl.BlockSpec((2 * PEER_HEADS, PEER_N_KEYS, PEER_HALF), lambda i: (0, 0, 0))],
        out_specs=[big(), big(), big(), big()],
        out_shape=[f32_shape, f32_shape, bf16_shape, bf16_shape],
        scratch_shapes=[pltpu.VMEM((2, PEER_TOPK, PEER_HEADS, tm), jnp.float32),
                        pltpu.VMEM((PEER_HEADS, PEER_N_KEYS, tm), jnp.float32)],
        compiler_params=_params("parallel"),
    )(x1b, wq_b, keys_b)


def _gelu_tanh(x):
    return 0.5 * x * (1.0 + jnp.tanh(math.sqrt(2.0 / math.pi) * (x + 0.044715 * (x * x * x))))


BF16_ROWS = 16


def _peer_dense_kernel(xb_ref, u_ref, v_ref, cnt_ref, e1_ref, rank2_ref, e2_ref, o_ref, acc_ref, wd_ref, *, tm, te):
    j = pl.program_id(1)
    n_tiles = pl.num_programs(1) - 1
    jc = jnp.minimum(j, n_tiles - 1)

    @pl.when(j == 0)
    def _():
        acc_ref[...] = jnp.zeros_like(acc_ref)
        wd_ref[...] = jnp.zeros_like(wd_ref)

    acc_ref[...] += _dot_tn(wd_ref[...], v_ref[...])
    hid = _dot_nt(u_ref[...], xb_ref[...])
    bf = jnp.bfloat16
    groups = PEER_N_KEYS // BF16_ROWS
    for k in range(te // PEER_N_KEYS):
        a = jc * (te // PEER_N_KEYS) + k
        gate = [jnp.zeros((BF16_ROWS, tm), bf) for _ in range(groups)]
        for h in range(PEER_HEADS):
            cnt = jnp.broadcast_to(cnt_ref[h, pl.ds(a, 1), :], (BF16_ROWS, tm)).astype(bf)
            e1 = jnp.broadcast_to(e1_ref[h, pl.ds(a, 1), :], (BF16_ROWS, tm)).astype(bf)
            for g in range(groups):
                rows = pl.ds(g * BF16_ROWS, BF16_ROWS)
                sel = rank2_ref[h, rows, :] < cnt
                gate[g] = gate[g] + jnp.where(sel, e1 * e2_ref[h, rows, :], jnp.zeros((), bf))
        for g in range(groups):
            r0 = k * PEER_N_KEYS + g * BF16_ROWS
            wd_ref[pl.ds(r0, BF16_ROWS), :] = gate[g] * _gelu_tanh(hid[r0:r0 + BF16_ROWS, :]).astype(bf)

    @pl.when(j == n_tiles)
    def _():
        o_ref[...] = acc_ref[...]


def _peer_dense(x1b, u_b, v_b, cnt, e1, rank2, e2):
    t = x1b.shape[0]
    n_exp = u_b.shape[0]
    tm, te = 512, 512
    n_tiles = n_exp // te
    big = lambda: pl.BlockSpec((PEER_HEADS, PEER_N_KEYS, tm), lambda i, j: (0, 0, i))
    row = lambda: pl.BlockSpec((tm, D_MODEL), lambda i, j: (i, 0))
    return pl.pallas_call(
        functools.partial(_peer_dense_kernel, tm=tm, te=te), name="peer_dense",
        grid=(t // tm, n_tiles + 1),
        in_specs=[row(),
                  pl.BlockSpec((te, D_MODEL), lambda i, j: (jnp.minimum(j, n_tiles - 1), 0)),
                  pl.BlockSpec((te, D_MODEL), lambda i, j: (jnp.maximum(j - 1, 0), 0)),
                  big(), big(), big(), big()],
        out_specs=row(),
        out_shape=jax.ShapeDtypeStruct((t, D_MODEL), jnp.float32),
        scratch_shapes=[pltpu.VMEM((tm, D_MODEL), jnp.float32), pltpu.VMEM((te, tm), jnp.bfloat16)],
        compiler_params=_params("parallel", "arbitrary"),
    )(x1b, u_b, v_b, cnt, e1, rank2, e2)


def _ple_kernel(x1_ref, po_ref, g_ref, b_ref, wg_ref, bg_ref, p_ref, wp_ref, o_ref, x2_ref, x2b_ref, *, tn):
    j = pl.program_id(1)

    @pl.when(j == 0)
    def _():
        x2 = _layer_norm(ALPHA * x1_ref[...] + po_ref[...], g_ref[...], b_ref[...])
        x2_ref[...] = x2
        x2b_ref[...] = x2.astype(jnp.bfloat16)

    gate = jax.nn.sigmoid(_dot(x2b_ref[...], wg_ref[...]) + bg_ref[...])
    col = pl.multiple_of(j * tn, tn)
    o_ref[...] = x2_ref[:, pl.ds(col, tn)] + gate * _dot(p_ref[...], wp_ref[...])


def _ple(x1, peer_out, ln_g, ln_b, w_gate_b, b_gate, p_b, w_proj_b):
    t = x1.shape[0]
    tm, tn = 512, 1024
    row = lambda: pl.BlockSpec((tm, D_MODEL), lambda i, j: (i, 0))
    vec = lambda: pl.BlockSpec((1, D_MODEL), lambda i, j: (0, 0))
    return pl.pallas_call(
        functools.partial(_ple_kernel, tn=tn), name="ple",
        grid=(t // tm, D_MODEL // tn),
        in_specs=[row(), row(), vec(), vec(),
                  pl.BlockSpec((D_MODEL, tn), lambda i, j: (0, j)),
                  pl.BlockSpec((1, tn), lambda i, j: (0, j)),
                  pl.BlockSpec((tm, P_DIM), lambda i, j: (i, 0)),
                  pl.BlockSpec((P_DIM, tn), lambda i, j: (0, j))],
        out_specs=pl.BlockSpec((tm, tn), lambda i, j: (i, j)),
        out_shape=jax.ShapeDtypeStruct((t, D_MODEL), jnp.float32),
        scratch_shapes=[pltpu.VMEM((tm, D_MODEL), jnp.float32), pltpu.VMEM((tm, D_MODEL), jnp.bfloat16)],
        compiler_params=_params("parallel", "arbitrary"),
    )(x1, peer_out, ln_g.reshape(1, -1), ln_b.reshape(1, -1), w_gate_b, b_gate.reshape(1, -1), p_b, w_proj_b)


def kernel(x, p, w_in, b_gate, conv_dw_w, conv_dw_b, conv_ln_g, conv_ln_b, conv_w_out, attn_lambda_q1, attn_lambda_k1, attn_lambda_q2, attn_lambda_k2, attn_subln_g, attn_w_o, w_out, ln1_g, ln1_b, peer_w_q, peer_sub_keys, peer_u, peer_v, ln2_g, ln2_b, ple_w_proj, ple_w_gate, ple_b_gate):
    batch, seq, d = x.shape
    t = batch * seq
    bf = jnp.bfloat16
    assert w_in.shape[0] == DEPTH == 1 and d == D_MODEL
    x2d = x.reshape(t, d)

    glu, qkv, gates = _input_projection(x2d.astype(bf), w_in[0].astype(bf), b_gate[0])
    y_conv = _conformer_conv(glu, conv_dw_w[0], conv_dw_b[0], conv_ln_g[0], conv_ln_b[0],
                             conv_w_out[0].astype(bf), batch, seq)
    o_norm = _diff_attention(qkv, attn_lambda_q1[0], attn_lambda_k1[0], attn_lambda_q2[0],
                             attn_lambda_k2[0], attn_subln_g[0], batch, seq)
    x1, x1b = _mix(gates, y_conv, o_norm.reshape(t, -1), attn_w_o[0].astype(bf), w_out[0].astype(bf),
                   x2d, ln1_g[0], ln1_b[0])

    wq = peer_w_q[0].reshape(d, PEER_HEADS, 2, PEER_HALF).transpose(0, 2, 1, 3).reshape(d, -1).astype(bf)
    keys = peer_sub_keys[0].transpose(1, 0, 2, 3).reshape(2 * PEER_HEADS, PEER_N_KEYS, PEER_HALF).astype(bf)
    cnt, e1, rank2, e2 = _peer_prep(x1b, wq, keys)
    peer_out = _peer_dense(x1b, peer_u[0].astype(bf), peer_v[0].astype(bf), cnt, e1, rank2, e2)

    out = _ple(x1, peer_out, ln2_g[0], ln2_b[0], ple_w_gate[0].astype(bf), ple_b_gate[0],
               p[0].reshape(t, P_DIM).astype(bf), ple_w_proj[0].astype(bf))
    return out.reshape(batch, seq, d)
```

```python
import functools
import math

import jax
import jax.numpy as jnp
import numpy as np
from jax import lax
from jax.experimental import pallas as pl
from jax.experimental.pallas import tpu as pltpu

D_MODEL = 2048
CONV_CH = 1024
CONV_WIDTH = 31
ATTN_HEADS = 8
ATTN_HEAD_DIM = 64
ATTN_V_DIM = 2 * ATTN_HEAD_DIM
PEER_HEADS = 8
PEER_N_KEYS = 128
PEER_HALF = 128
PEER_TOPK = 16
P_DIM = 256
LN_EPS = 1e-5
DEPTH = 1
ALPHA = (2.0 * DEPTH) ** 0.25
LAM_INIT = 0.8 - 0.6 * math.exp(-0.3 * 0)

COL_CONV = 2 * CONV_CH
COL_QKV = 3 * ATTN_HEADS * ATTN_V_DIM
COL_GATE = 2 * D_MODEL

VMEM_LIMIT_BYTES = 56 * 1024 * 1024
NEG_INF = float("-inf")

CAND_PAIRS = tuple((i, j) for i in range(PEER_TOPK) for j in range(PEER_TOPK)
                   if (i + 1) * (j + 1) <= PEER_TOPK)


def _params(*sem, **kw):
    return pltpu.CompilerParams(dimension_semantics=sem, vmem_limit_bytes=VMEM_LIMIT_BYTES, **kw)


def _dot(a, b):
    return jnp.dot(a, b, preferred_element_type=jnp.float32)


def _dot_nt(a, b):
    return lax.dot_general(a, b, (((1,), (1,)), ((), ())), preferred_element_type=jnp.float32)


def _dot_tn(a, b):
    return lax.dot_general(a, b, (((0,), (0,)), ((), ())), preferred_element_type=jnp.float32)


def _layer_norm(r, g, b):
    mu = jnp.mean(r, axis=-1, keepdims=True)
    c = r - mu
    var = jnp.mean(c * c, axis=-1, keepdims=True)
    return c * lax.rsqrt(var + LN_EPS) * g + b


def _proj_glu_kernel(x_ref, wa_ref, wb_ref, o_ref):
    x = x_ref[...]
    o_ref[...] = _dot(x, wa_ref[...]) * jax.nn.sigmoid(_dot(x, wb_ref[...]))


def _proj_qkv_kernel(x_ref, w_ref, o_ref):
    o_ref[...] = _dot(x_ref[...], w_ref[...]).astype(o_ref.dtype)


def _proj_gate_kernel(x_ref, w_ref, b_ref, o_ref):
    o_ref[...] = jax.nn.sigmoid(_dot(x_ref[...], w_ref[...]) + b_ref[...])


def _input_projection(xb, w_in_b, b_gate):
    t = xb.shape[0]
    tm = 1024
    x_spec = pl.BlockSpec((tm, D_MODEL), lambda i, j: (i, 0))

    def w_spec(col0, tn):
        return pl.BlockSpec((D_MODEL, tn), lambda i, j: (0, j + col0 // tn))

    def o_spec(tn):
        return pl.BlockSpec((tm, tn), lambda i, j: (i, j))

    tn = 512
    glu = pl.pallas_call(
        _proj_glu_kernel, name="proj_glu",
        grid=(t // tm, CONV_CH // tn),
        in_specs=[x_spec, w_spec(0, tn), w_spec(CONV_CH, tn)],
        out_specs=o_spec(tn),
        out_shape=jax.ShapeDtypeStruct((t, CONV_CH), jnp.float32),
        compiler_params=_params("parallel", "parallel"),
    )(xb, w_in_b, w_in_b)
    tn = 1024
    qkv = pl.pallas_call(
        _proj_qkv_kernel, name="proj_qkv",
        grid=(t // tm, COL_QKV // tn),
        in_specs=[x_spec, w_spec(COL_CONV, tn)],
        out_specs=o_spec(tn),
        out_shape=jax.ShapeDtypeStruct((t, COL_QKV), jnp.bfloat16),
        compiler_params=_params("parallel", "parallel"),
    )(xb, w_in_b)
    gates = pl.pallas_call(
        _proj_gate_kernel, name="proj_gate",
        grid=(t // tm, COL_GATE // tn),
        in_specs=[x_spec, w_spec(COL_CONV + COL_QKV, tn), pl.BlockSpec((1, tn), lambda i, j: (0, j))],
        out_specs=o_spec(tn),
        out_shape=jax.ShapeDtypeStruct((t, COL_GATE), jnp.float32),
        compiler_params=_params("parallel", "parallel"),
    )(xb, w_in_b, b_gate.reshape(1, COL_GATE))
    return glu, qkv, gates


SUBLANES = 8
CONV_HALO = 32


def _conv_kernel(prev_ref, cur_ref, dww_ref, dwb_ref, lng_ref, lnb_ref, wpw_ref, o_ref, ext_ref, *, ts):
    i = pl.program_id(1)
    halo = prev_ref[0]
    ext_ref[pl.ds(0, CONV_HALO), :] = jnp.where(i > 0, halo, jnp.zeros_like(halo))
    ext_ref[pl.ds(CONV_HALO, ts), :] = cur_ref[0]
    off0 = CONV_HALO - (CONV_WIDTH - 1)
    acc = jnp.zeros((ts, CONV_CH), jnp.float32)
    ext = ext_ref[...]
    n_ext = CONV_HALO + ts
    for r in range(SUBLANES):
        offsets = [o for o in range(r, CONV_HALO + 1, SUBLANES) if 0 <= o - off0 < CONV_WIDTH]
        shifted = ext if r == 0 else pltpu.roll(ext, n_ext - r, axis=0)
        for o in offsets:
            acc = acc + shifted[o - r:o - r + ts, :] * dww_ref[pl.ds(o - off0, 1), :]
    acc = acc + dwb_ref[...]
    y = _layer_norm(acc, lng_ref[...], lnb_ref[...])
    y = y * jax.nn.sigmoid(y)
    o_ref[...] = _dot(y.astype(jnp.bfloat16), wpw_ref[...])


def _conformer_conv(glu, dw_w, dw_b, ln_g, ln_b, w_pw_b, batch, seq):
    ts = 256
    h3 = glu.reshape(batch, seq, CONV_CH)
    nblk = ts // CONV_HALO
    vec = lambda: pl.BlockSpec((1, CONV_CH), lambda b, i: (0, 0))
    return pl.pallas_call(
        functools.partial(_conv_kernel, ts=ts), name="conv_module",
        grid=(batch, seq // ts),
        in_specs=[
            pl.BlockSpec((1, CONV_HALO, CONV_CH), lambda b, i: (b, jnp.maximum(i * nblk - 1, 0), 0)),
            pl.BlockSpec((1, ts, CONV_CH), lambda b, i: (b, i, 0)),
            pl.BlockSpec((CONV_WIDTH, CONV_CH), lambda b, i: (0, 0)),
            vec(), vec(), vec(),
            pl.BlockSpec((CONV_CH, D_MODEL), lambda b, i: (0, 0)),
        ],
        out_specs=pl.BlockSpec((ts, D_MODEL), lambda b, i: (b * (seq // ts) + i, 0)),
        out_shape=jax.ShapeDtypeStruct((batch * seq, D_MODEL), jnp.float32),
        scratch_shapes=[pltpu.VMEM((CONV_HALO + ts, CONV_CH), jnp.float32)],
        compiler_params=_params("parallel", "parallel"),
    )(h3, h3, dw_w.reshape(CONV_WIDTH, CONV_CH), dw_b.reshape(1, CONV_CH),
      ln_g.reshape(1, CONV_CH), ln_b.reshape(1, CONV_CH), w_pw_b)


def _attn_kernel(slopes_ref, q_ref, k_ref, v_ref, lq1_ref, lk1_ref, lq2_ref, lk2_ref, g_ref, o_ref, *, tq):
    h = pl.program_id(1)
    qi = pl.program_id(2)
    slope = slopes_ref[h]
    q = q_ref[0] * (ATTN_HEAD_DIM ** -0.5)
    lane = lax.broadcasted_iota(jnp.int32, q.shape, 1)
    zero = jnp.zeros_like(q)
    q1 = jnp.where(lane < ATTN_HEAD_DIM, q, zero)
    q2 = jnp.where(lane >= ATTN_HEAD_DIM, q, zero)
    causal = (lax.broadcasted_iota(jnp.int32, (tq, tq), 0)
              >= lax.broadcasted_iota(jnp.int32, (tq, tq), 1))
    kcol = lax.broadcasted_iota(jnp.int32, (1, tq), 1)

    def block(kb, carry, diagonal):
        m1, l1, a1, m2, l2, a2 = carry
        k = k_ref[0, pl.ds(kb * tq, tq), :]
        v = v_ref[0, pl.ds(kb * tq, tq), :]
        bias = slope * ((kb - qi) * tq + kcol).astype(jnp.float32)

        def one(qm, m, l, a):
            s = _dot_nt(qm, k) + bias
            if diagonal:
                s = jnp.where(causal, s, NEG_INF)
            m_new = jnp.maximum(m, jnp.max(s, axis=-1, keepdims=True))
            p = jnp.exp(s - m_new)
            corr = jnp.exp(m - m_new)
            l_new = corr * l + jnp.sum(p, axis=-1, keepdims=True)
            a_new = corr * a + _dot(p.astype(v.dtype), v)
            return m_new, l_new, a_new

        m1, l1, a1 = one(q1, m1, l1, a1)
        m2, l2, a2 = one(q2, m2, l2, a2)
        return m1, l1, a1, m2, l2, a2

    col = lambda val: jnp.full((tq, 1), val, jnp.float32)
    acc0 = jnp.zeros((tq, ATTN_V_DIM), jnp.float32)
    init = (col(NEG_INF), col(0.0), acc0, col(NEG_INF), col(0.0), acc0)
    carry = lax.fori_loop(0, qi, functools.partial(block, diagonal=False), init)
    m1, l1, a1, m2, l2, a2 = block(qi, carry, diagonal=True)

    lam = (jnp.exp(jnp.sum(lq1_ref[...] * lk1_ref[...], axis=-1, keepdims=True))
           - jnp.exp(jnp.sum(lq2_ref[...] * lk2_ref[...], axis=-1, keepdims=True)) + LAM_INIT)
    o = a1 / l1 - lam * (a2 / l2)
    ms = jnp.mean(o * o, axis=-1, keepdims=True)
    o = o * lax.rsqrt(ms + LN_EPS) * g_ref[...] * (1.0 - LAM_INIT)
    o_ref[0] = o.astype(o_ref.dtype)


def _diff_attention(qkv, lq1, lk1, lq2, lk2, subln_g, batch, seq):
    tq = 512
    qkv3 = qkv.reshape(batch, seq, COL_QKV)
    slopes = jnp.asarray(2.0 ** (-8.0 * np.arange(1, ATTN_HEADS + 1) / ATTN_HEADS), jnp.float32)
    hd = lambda: pl.BlockSpec((1, ATTN_HEAD_DIM), lambda b, h, i: (0, 0))
    return pl.pallas_call(
        functools.partial(_attn_kernel, tq=tq), name="diff_attention",
        grid=(batch, ATTN_HEADS, seq // tq),
        in_specs=[
            pl.BlockSpec(memory_space=pltpu.SMEM),
            pl.BlockSpec((1, tq, ATTN_V_DIM), lambda b, h, i: (b, i, h)),
            pl.BlockSpec((1, seq, ATTN_V_DIM), lambda b, h, i: (b, 0, ATTN_HEADS + h)),
            pl.BlockSpec((1, seq, ATTN_V_DIM), lambda b, h, i: (b, 0, 2 * ATTN_HEADS + h)),
            hd(), hd(), hd(), hd(),
            pl.BlockSpec((1, ATTN_V_DIM), lambda b, h, i: (0, 0)),
        ],
        out_specs=pl.BlockSpec((1, tq, ATTN_V_DIM), lambda b, h, i: (b, i, h)),
        out_shape=jax.ShapeDtypeStruct((batch, seq, ATTN_HEADS * ATTN_V_DIM), jnp.bfloat16),
        compiler_params=_params("parallel", "parallel", "arbitrary"),
    )(slopes, qkv3, qkv3, qkv3, lq1.reshape(1, -1), lk1.reshape(1, -1), lq2.reshape(1, -1),
      lk2.reshape(1, -1), subln_g.reshape(1, -1))


def _mix_kernel(gate_ref, yc_ref, on_ref, wo_ref, wout_ref, x_ref, g_ref, b_ref, x1_ref, x1b_ref):
    y_attn = _dot(on_ref[...], wo_ref[...])
    mixed = gate_ref[:, :D_MODEL] * yc_ref[...] + gate_ref[:, D_MODEL:] * y_attn
    z = _dot(mixed.astype(jnp.bfloat16), wout_ref[...])
    x1 = _layer_norm(ALPHA * x_ref[...] + z, g_ref[...], b_ref[...])
    x1_ref[...] = x1
    x1b_ref[...] = x1.astype(jnp.bfloat16)


def _mix(gates, y_conv, o_norm, w_o_b, w_out_b, x2d, ln_g, ln_b):
    t = x2d.shape[0]
    tm = 256
    row = lambda n: pl.BlockSpec((tm, n), lambda i: (i, 0))
    full = lambda a, b: pl.BlockSpec((a, b), lambda i: (0, 0))
    return pl.pallas_call(
        _mix_kernel, name="mix_deepnorm",
        grid=(t // tm,),
        in_specs=[row(COL_GATE), row(D_MODEL), row(ATTN_HEADS * ATTN_V_DIM),
                  full(ATTN_HEADS * ATTN_V_DIM, D_MODEL), full(D_MODEL, D_MODEL), row(D_MODEL),
                  full(1, D_MODEL), full(1, D_MODEL)],
        out_specs=[row(D_MODEL), row(D_MODEL)],
        out_shape=[jax.ShapeDtypeStruct((t, D_MODEL), jnp.float32),
                   jax.ShapeDtypeStruct((t, D_MODEL), jnp.bfloat16)],
        compiler_params=_params("parallel"),
    )(gates, y_conv, o_norm, w_o_b, w_out_b, x2d, ln_g.reshape(1, -1), ln_b.reshape(1, -1))


def _peer_prep_kernel(x_ref, wq_ref, keys_ref, cnt_ref, e1_ref, rank2_ref, e2_ref, top_ref, s1_ref, *, tm):
    q = _dot(x_ref[...], wq_ref[...]).astype(jnp.bfloat16)
    for c in range(2):
        for h in range(PEER_HEADS):
            idx = h * 2 + c
            s = _dot_nt(keys_ref[idx], q[:, idx * PEER_HALF:(idx + 1) * PEER_HALF])
            work = s
            rank = jnp.full(s.shape, float(PEER_TOPK), jnp.float32)
            for r in range(PEER_TOPK):
                mx = jnp.max(work, axis=0, keepdims=True)
                top_ref[c, r, pl.ds(h, 1), :] = mx
                hit = work == mx
                if c == 1:
                    rank = jnp.where(hit, float(r), rank)
                work = jnp.where(hit, NEG_INF, work)
            if c == 0:
                s1_ref[h] = s
            else:
                rank2_ref[h] = rank.astype(rank2_ref.dtype)
                e2_ref[h] = jnp.exp(s - top_ref[1, 0, pl.ds(h, 1), :]).astype(e2_ref.dtype)
    cands = [top_ref[0, i] + top_ref[1, j] for (i, j) in CAND_PAIRS]
    work = list(cands)
    remaining = jnp.full((PEER_HEADS, tm), float(PEER_TOPK), jnp.float32)
    tau = jnp.full((PEER_HEADS, tm), NEG_INF, jnp.float32)
    for _ in range(PEER_TOPK):
        mx = functools.reduce(jnp.maximum, work)
        hit = [w == mx for w in work]
        cnt = functools.reduce(jnp.add, [hm.astype(jnp.float32) for hm in hit])
        newly = jnp.logical_and(remaining > 0.0, cnt >= remaining)
        tau = jnp.where(newly, mx, tau)
        remaining = remaining - cnt
        work = [jnp.where(hm, NEG_INF, w) for hm, w in zip(hit, work)]
    m1 = top_ref[0, 0]
    m2 = top_ref[1, 0]
    ex1 = [jnp.exp(top_ref[0, i] - m1) for i in range(PEER_TOPK)]
    ex2 = [jnp.exp(top_ref[1, j] - m2) for j in range(PEER_TOPK)]
    z = jnp.zeros((PEER_HEADS, tm), jnp.float32)
    sel_count = [jnp.zeros((PEER_HEADS, tm), jnp.float32) for _ in range(PEER_TOPK)]
    for cand, (i, j) in zip(cands, CAND_PAIRS):
        sel = cand >= tau
        z = z + jnp.where(sel, ex1[i] * ex2[j], 0.0)
        sel_count[i] = sel_count[i] + sel.astype(jnp.float32)
    for h in range(PEER_HEADS):
        s1 = s1_ref[h]
        cnt = jnp.zeros(s1.shape, jnp.float32)
        for i in range(PEER_TOPK):
            cnt = jnp.where(s1 == top_ref[0, i, pl.ds(h, 1), :], sel_count[i][h:h + 1, :], cnt)
        cnt_ref[h] = cnt
        e1_ref[h] = jnp.exp(s1 - m1[h:h + 1, :]) / z[h:h + 1, :]


def _peer_prep(x1b, wq_b, keys_b):
    t = x1b.shape[0]
    tm = 256
    nq = 2 * PEER_HEADS * PEER_HALF
    big = lambda: pl.BlockSpec((PEER_HEADS, PEER_N_KEYS, tm), lambda i: (0, 0, i))
    f32_shape = jax.ShapeDtypeStruct((PEER_HEADS, PEER_N_KEYS, t), jnp.float32)
    bf16_shape = jax.ShapeDtypeStruct((PEER_HEADS, PEER_N_KEYS, t), jnp.bfloat16)
    return pl.pallas_call(
        functools.partial(_peer_prep_kernel, tm=tm), name="peer_prep",
        grid=(t // tm,),
        in_specs=[pl.BlockSpec((tm, D_MODEL), lambda i: (i, 0)),
                  pl.BlockSpec((D_MODEL, nq), lambda i: (0, 0)),
                  pl.BlockSpec((2 * PEER_HEADS, PEER_N_KEYS, PEER_HALF), lambda i: (0, 0, 0))],
        out_specs=[big(), big(), big(), big()],
        out_shape=[f32_shape, f32_shape, bf16_shape, bf16_shape],
        scratch_shapes=[pltpu.VMEM((2, PEER_TOPK, PEER_HEADS, tm), jnp.float32),
                        pltpu.VMEM((PEER_HEADS, PEER_N_KEYS, tm), jnp.float32)],
        compiler_params=_params("parallel"),
    )(x1b, wq_b, keys_b)


def _gelu_tanh(x):
    c = math.sqrt(2.0 / math.pi)
    half = 0.5 * x
    return half + half * jnp.tanh(x * (c + (c * 0.044715) * (x * x)))


BF16_ROWS = 16


def _peer_dense_kernel(xb_ref, u_ref, v_ref, cnt_ref, e1_ref, rank2_ref, e2_ref, o_ref, acc_ref, wd_ref, *, tm, te):
    j = pl.program_id(1)
    n_tiles = pl.num_programs(1) - 1
    jc = jnp.minimum(j, n_tiles - 1)

    @pl.when(j == 0)
    def _():
        acc_ref[...] = jnp.zeros_like(acc_ref)
        wd_ref[...] = jnp.zeros_like(wd_ref)

    acc_ref[...] += _dot_tn(wd_ref[...], v_ref[...])
    hid = _dot_nt(u_ref[...], xb_ref[...])
    bf = jnp.bfloat16
    groups = PEER_N_KEYS // BF16_ROWS
    for k in range(te // PEER_N_KEYS):
        a = jc * (te // PEER_N_KEYS) + k
        gate = [jnp.zeros((BF16_ROWS, tm), bf) for _ in range(groups)]
        for h in range(PEER_HEADS):
            cnt = jnp.broadcast_to(cnt_ref[h, pl.ds(a, 1), :], (BF16_ROWS, tm)).astype(bf)
            e1 = jnp.broadcast_to(e1_ref[h, pl.ds(a, 1), :], (BF16_ROWS, tm)).astype(bf)
            for g in range(groups):
                rows = pl.ds(g * BF16_ROWS, BF16_ROWS)
                sel = rank2_ref[h, rows, :] < cnt
                gate[g] = gate[g] + jnp.where(sel, e1 * e2_ref[h, rows, :], jnp.zeros((), bf))
        for g in range(groups):
            r0 = k * PEER_N_KEYS + g * BF16_ROWS
            wd_ref[pl.ds(r0, BF16_ROWS), :] = gate[g] * _gelu_tanh(hid[r0:r0 + BF16_ROWS, :]).astype(bf)

    @pl.when(j == n_tiles)
    def _():
        o_ref[...] = acc_ref[...]


def _peer_dense(x1b, u_b, v_b, cnt, e1, rank2, e2):
    t = x1b.shape[0]
    n_exp = u_b.shape[0]
    tm, te = 512, 512
    n_tiles = n_exp // te
    big = lambda: pl.BlockSpec((PEER_HEADS, PEER_N_KEYS, tm), lambda i, j: (0, 0, i))
    row = lambda: pl.BlockSpec((tm, D_MODEL), lambda i, j: (i, 0))
    return pl.pallas_call(
        functools.partial(_peer_dense_kernel, tm=tm, te=te), name="peer_dense",
        grid=(t // tm, n_tiles + 1),
        in_specs=[row(),
                  pl.BlockSpec((te, D_MODEL), lambda i, j: (jnp.minimum(j, n_tiles - 1), 0)),
                  pl.BlockSpec((te, D_MODEL), lambda i, j: (jnp.maximum(j - 1, 0), 0)),
                  big(), big(), big(), big()],
        out_specs=row(),
        out_shape=jax.ShapeDtypeStruct((t, D_MODEL), jnp.float32),
        scratch_shapes=[pltpu.VMEM((tm, D_MODEL), jnp.float32), pltpu.VMEM((te, tm), jnp.bfloat16)],
        compiler_params=_params("parallel", "arbitrary"),
    )(x1b, u_b, v_b, cnt, e1, rank2, e2)


def _ple_kernel(x1_ref, po_ref, g_ref, b_ref, wg_ref, bg_ref, p_ref, wp_ref, o_ref):
    x2 = _layer_norm(ALPHA * x1_ref[...] + po_ref[...], g_ref[...], b_ref[...])
    gate = jax.nn.sigmoid(_dot(x2.astype(jnp.bfloat16), wg_ref[...]) + bg_ref[...])
    o_ref[...] = x2 + gate * _dot(p_ref[...], wp_ref[...])


def _ple(x1, peer_out, ln_g, ln_b, w_gate_b, b_gate, p_b, w_proj_b):
    t = x1.shape[0]
    tm = 512
    row = lambda: pl.BlockSpec((tm, D_MODEL), lambda i: (i, 0))
    vec = lambda: pl.BlockSpec((1, D_MODEL), lambda i: (0, 0))
    return pl.pallas_call(
        _ple_kernel, name="ple",
        grid=(t // tm,),
        in_specs=[row(), row(), vec(), vec(),
                  pl.BlockSpec((D_MODEL, D_MODEL), lambda i: (0, 0)),
                  vec(),
                  pl.BlockSpec((tm, P_DIM), lambda i: (i, 0)),
                  pl.BlockSpec((P_DIM, D_MODEL), lambda i: (0, 0))],
        out_specs=row(),
        out_shape=jax.ShapeDtypeStruct((t, D_MODEL), jnp.float32),
        compiler_params=_params("parallel"),
    )(x1, peer_out, ln_g.reshape(1, -1), ln_b.reshape(1, -1), w_gate_b, b_gate.reshape(1, -1), p_b, w_proj_b)


def kernel(x, p, w_in, b_gate, conv_dw_w, conv_dw_b, conv_ln_g, conv_ln_b, conv_w_out, attn_lambda_q1, attn_lambda_k1, attn_lambda_q2, attn_lambda_k2, attn_subln_g, attn_w_o, w_out, ln1_g, ln1_b, peer_w_q, peer_sub_keys, peer_u, peer_v, ln2_g, ln2_b, ple_w_proj, ple_w_gate, ple_b_gate):
    batch, seq, d = x.shape
    t = batch * seq
    bf = jnp.bfloat16
    assert w_in.shape[0] == DEPTH == 1 and d == D_MODEL
    x2d = x.reshape(t, d)

    glu, qkv, gates = _input_projection(x2d.astype(bf), w_in[0].astype(bf), b_gate[0])
    y_conv = _conformer_conv(glu, conv_dw_w[0], conv_dw_b[0], conv_ln_g[0], conv_ln_b[0],
                             conv_w_out[0].astype(bf), batch, seq)
    o_norm = _diff_attention(qkv, attn_lambda_q1[0], attn_lambda_k1[0], attn_lambda_q2[0],
                             attn_lambda_k2[0], attn_subln_g[0], batch, seq)
    x1, x1b = _mix(gates, y_conv, o_norm.reshape(t, -1), attn_w_o[0].astype(bf), w_out[0].astype(bf),
                   x2d, ln1_g[0], ln1_b[0])

    wq = peer_w_q[0].astype(bf)
    keys = peer_sub_keys[0].reshape(2 * PEER_HEADS, PEER_N_KEYS, PEER_HALF).astype(bf)
    cnt, e1, rank2, e2 = _peer_prep(x1b, wq, keys)
    peer_out = _peer_dense(x1b, peer_u[0].astype(bf), peer_v[0].astype(bf), cnt, e1, rank2, e2)

    out = _ple(x1, peer_out, ln2_g[0], ln2_b[0], ple_w_gate[0].astype(bf), ple_b_gate[0],
               p[0].reshape(t, P_DIM).astype(bf), ple_w_proj[0].astype(bf))
    return out.reshape(batch, seq, d)
```

```python
import functools
import math

import jax
import jax.numpy as jnp
import numpy as np
from jax import lax
from jax.experimental import pallas as pl
from jax.experimental.pallas import tpu as pltpu

D_MODEL = 2048
CONV_CH = 1024
CONV_WIDTH = 31
ATTN_HEADS = 8
ATTN_HEAD_DIM = 64
ATTN_V_DIM = 2 * ATTN_HEAD_DIM
PEER_HEADS = 8
PEER_N_KEYS = 128
PEER_HALF = 128
PEER_TOPK = 16
P_DIM = 256
LN_EPS = 1e-5
DEPTH = 1
ALPHA = (2.0 * DEPTH) ** 0.25
LAM_INIT = 0.8 - 0.6 * math.exp(-0.3 * 0)

COL_CONV = 2 * CONV_CH
COL_QKV = 3 * ATTN_HEADS * ATTN_V_DIM
COL_GATE = 2 * D_MODEL

VMEM_LIMIT_BYTES = 56 * 1024 * 1024
NEG_INF = float("-inf")

CAND_PAIRS = tuple((i, j) for i in range(PEER_TOPK) for j in range(PEER_TOPK)
                   if (i + 1) * (j + 1) <= PEER_TOPK)


def _params(*sem, **kw):
    return pltpu.CompilerParams(dimension_semantics=sem, vmem_limit_bytes=VMEM_LIMIT_BYTES, **kw)


def _dot(a, b):
    return jnp.dot(a, b, preferred_element_type=jnp.float32)


def _dot_nt(a, b):
    return lax.dot_general(a, b, (((1,), (1,)), ((), ())), preferred_element_type=jnp.float32)


def _dot_tn(a, b):
    return lax.dot_general(a, b, (((0,), (0,)), ((), ())), preferred_element_type=jnp.float32)


def _layer_norm(r, g, b):
    mu = jnp.mean(r, axis=-1, keepdims=True)
    c = r - mu
    var = jnp.mean(c * c, axis=-1, keepdims=True)
    return c * lax.rsqrt(var + LN_EPS) * g + b


def _proj_glu_kernel(x_ref, wa_ref, wb_ref, o_ref):
    x = x_ref[...]
    o_ref[...] = _dot(x, wa_ref[...]) * jax.nn.sigmoid(_dot(x, wb_ref[...]))


def _proj_qkv_kernel(x_ref, w_ref, o_ref):
    o_ref[...] = _dot(x_ref[...], w_ref[...]).astype(o_ref.dtype)


def _proj_gate_kernel(x_ref, w_ref, b_ref, o_ref):
    o_ref[...] = jax.nn.sigmoid(_dot(x_ref[...], w_ref[...]) + b_ref[...])


def _input_projection(xb, w_in_b, b_gate):
    t = xb.shape[0]
    tm = 1024
    x_spec = pl.BlockSpec((tm, D_MODEL), lambda i, j: (i, 0))

    def w_spec(col0, tn):
        return pl.BlockSpec((D_MODEL, tn), lambda i, j: (0, j + col0 // tn))

    def o_spec(tn):
        return pl.BlockSpec((tm, tn), lambda i, j: (i, j))

    tn = 512
    glu = pl.pallas_call(
        _proj_glu_kernel, name="proj_glu",
        grid=(t // tm, CONV_CH // tn),
        in_specs=[x_spec, w_spec(0, tn), w_spec(CONV_CH, tn)],
        out_specs=o_spec(tn),
        out_shape=jax.ShapeDtypeStruct((t, CONV_CH), jnp.float32),
        compiler_params=_params("parallel", "parallel"),
    )(xb, w_in_b, w_in_b)
    tn = 1024
    qkv = pl.pallas_call(
        _proj_qkv_kernel, name="proj_qkv",
        grid=(t // tm, COL_QKV // tn),
        in_specs=[x_spec, w_spec(COL_CONV, tn)],
        out_specs=o_spec(tn),
        out_shape=jax.ShapeDtypeStruct((t, COL_QKV), jnp.bfloat16),
        compiler_params=_params("parallel", "parallel"),
    )(xb, w_in_b)
    gates = pl.pallas_call(
        _proj_gate_kernel, name="proj_gate",
        grid=(t // tm, COL_GATE // tn),
        in_specs=[x_spec, w_spec(COL_CONV + COL_QKV, tn), pl.BlockSpec((1, tn), lambda i, j: (0, j))],
        out_specs=o_spec(tn),
        out_shape=jax.ShapeDtypeStruct((t, COL_GATE), jnp.float32),
        compiler_params=_params("parallel", "parallel"),
    )(xb, w_in_b, b_gate.reshape(1, COL_GATE))
    return glu, qkv, gates


SUBLANES = 8
CONV_HALO = 32


def _conv_kernel(prev_ref, cur_ref, dww_ref, dwb_ref, lng_ref, lnb_ref, wpw_ref, o_ref, ext_ref, *, ts):
    i = pl.program_id(1)
    halo = prev_ref[0]
    ext_ref[pl.ds(0, CONV_HALO), :] = jnp.where(i > 0, halo, jnp.zeros_like(halo))
    ext_ref[pl.ds(CONV_HALO, ts), :] = cur_ref[0]
    off0 = CONV_HALO - (CONV_WIDTH - 1)
    acc = jnp.zeros((ts, CONV_CH), jnp.float32)
    ext = ext_ref[...]
    n_ext = CONV_HALO + ts
    for r in range(SUBLANES):
        offsets = [o for o in range(r, CONV_HALO + 1, SUBLANES) if 0 <= o - off0 < CONV_WIDTH]
        shifted = ext if r == 0 else pltpu.roll(ext, n_ext - r, axis=0)
        for o in offsets:
            acc = acc + shifted[o - r:o - r + ts, :] * dww_ref[pl.ds(o - off0, 1), :]
    acc = acc + dwb_ref[...]
    y = _layer_norm(acc, lng_ref[...], lnb_ref[...])
    y = y * jax.nn.sigmoid(y)
    o_ref[...] = _dot(y.astype(jnp.bfloat16), wpw_ref[...])


def _conformer_conv(glu, dw_w, dw_b, ln_g, ln_b, w_pw_b, batch, seq):
    ts = 256
    h3 = glu.reshape(batch, seq, CONV_CH)
    nblk = ts // CONV_HALO
    vec = lambda: pl.BlockSpec((1, CONV_CH), lambda b, i: (0, 0))
    return pl.pallas_call(
        functools.partial(_conv_kernel, ts=ts), name="conv_module",
        grid=(batch, seq // ts),
        in_specs=[
            pl.BlockSpec((1, CONV_HALO, CONV_CH), lambda b, i: (b, jnp.maximum(i * nblk - 1, 0), 0)),
            pl.BlockSpec((1, ts, CONV_CH), lambda b, i: (b, i, 0)),
            pl.BlockSpec((CONV_WIDTH, CONV_CH), lambda b, i: (0, 0)),
            vec(), vec(), vec(),
            pl.BlockSpec((CONV_CH, D_MODEL), lambda b, i: (0, 0)),
        ],
        out_specs=pl.BlockSpec((ts, D_MODEL), lambda b, i: (b * (seq // ts) + i, 0)),
        out_shape=jax.ShapeDtypeStruct((batch * seq, D_MODEL), jnp.float32),
        scratch_shapes=[pltpu.VMEM((CONV_HALO + ts, CONV_CH), jnp.float32)],
        compiler_params=_params("parallel", "parallel"),
    )(h3, h3, dw_w.reshape(CONV_WIDTH, CONV_CH), dw_b.reshape(1, CONV_CH),
      ln_g.reshape(1, CONV_CH), ln_b.reshape(1, CONV_CH), w_pw_b)


def _attn_kernel(slopes_ref, q_ref, k_ref, v_ref, lq1_ref, lk1_ref, lq2_ref, lk2_ref, g_ref, o_ref, *, tq):
    h = pl.program_id(1)
    qi = pl.program_id(2)
    slope = slopes_ref[h]
    q = q_ref[0] * (ATTN_HEAD_DIM ** -0.5)
    lane = lax.broadcasted_iota(jnp.int32, q.shape, 1)
    zero = jnp.zeros_like(q)
    q1 = jnp.where(lane < ATTN_HEAD_DIM, q, zero)
    q2 = jnp.where(lane >= ATTN_HEAD_DIM, q, zero)
    causal = (lax.broadcasted_iota(jnp.int32, (tq, tq), 0)
              >= lax.broadcasted_iota(jnp.int32, (tq, tq), 1))
    kcol = lax.broadcasted_iota(jnp.int32, (1, tq), 1)

    def block(kb, carry, diagonal):
        m1, l1, a1, m2, l2, a2 = carry
        k = k_ref[0, pl.ds(kb * tq, tq), :]
        v = v_ref[0, pl.ds(kb * tq, tq), :]
        bias = slope * ((kb - qi) * tq + kcol).astype(jnp.float32)

        def one(qm, m, l, a):
            s = _dot_nt(qm, k) + bias
            if diagonal:
                s = jnp.where(causal, s, NEG_INF)
            m_new = jnp.maximum(m, jnp.max(s, axis=-1, keepdims=True))
            p = jnp.exp(s - m_new)
            corr = jnp.exp(m - m_new)
            l_new = corr * l + jnp.sum(p, axis=-1, keepdims=True)
            a_new = corr * a + _dot(p.astype(v.dtype), v)
            return m_new, l_new, a_new

        m1, l1, a1 = one(q1, m1, l1, a1)
        m2, l2, a2 = one(q2, m2, l2, a2)
        return m1, l1, a1, m2, l2, a2

    col = lambda val: jnp.full((tq, 1), val, jnp.float32)
    acc0 = jnp.zeros((tq, ATTN_V_DIM), jnp.float32)
    init = (col(NEG_INF), col(0.0), acc0, col(NEG_INF), col(0.0), acc0)
    carry = lax.fori_loop(0, qi, functools.partial(block, diagonal=False), init)
    m1, l1, a1, m2, l2, a2 = block(qi, carry, diagonal=True)

    lam = (jnp.exp(jnp.sum(lq1_ref[...] * lk1_ref[...], axis=-1, keepdims=True))
           - jnp.exp(jnp.sum(lq2_ref[...] * lk2_ref[...], axis=-1, keepdims=True)) + LAM_INIT)
    o = a1 / l1 - lam * (a2 / l2)
    ms = jnp.mean(o * o, axis=-1, keepdims=True)
    o = o * lax.rsqrt(ms + LN_EPS) * g_ref[...] * (1.0 - LAM_INIT)
    o_ref[0] = o.astype(o_ref.dtype)


def _diff_attention(qkv, lq1, lk1, lq2, lk2, subln_g, batch, seq):
    tq = 1024
    qkv3 = qkv.reshape(batch, seq, COL_QKV)
    slopes = jnp.asarray(2.0 ** (-8.0 * np.arange(1, ATTN_HEADS + 1) / ATTN_HEADS), jnp.float32)
    hd = lambda: pl.BlockSpec((1, ATTN_HEAD_DIM), lambda b, h, i: (0, 0))
    return pl.pallas_call(
        functools.partial(_attn_kernel, tq=tq), name="diff_attention",
        grid=(batch, ATTN_HEADS, seq // tq),
        in_specs=[
            pl.BlockSpec(memory_space=pltpu.SMEM),
            pl.BlockSpec((1, tq, ATTN_V_DIM), lambda b, h, i: (b, i, h)),
            pl.BlockSpec((1, seq, ATTN_V_DIM), lambda b, h, i: (b, 0, ATTN_HEADS + h)),
            pl.BlockSpec((1, seq, ATTN_V_DIM), lambda b, h, i: (b, 0, 2 * ATTN_HEADS + h)),
            hd(), hd(), hd(), hd(),
            pl.BlockSpec((1, ATTN_V_DIM), lambda b, h, i: (0, 0)),
        ],
        out_specs=pl.BlockSpec((1, tq, ATTN_V_DIM), lambda b, h, i: (b, i, h)),
        out_shape=jax.ShapeDtypeStruct((batch, seq, ATTN_HEADS * ATTN_V_DIM), jnp.bfloat16),
        compiler_params=_params("parallel", "parallel", "arbitrary"),
    )(slopes, qkv3, qkv3, qkv3, lq1.reshape(1, -1), lk1.reshape(1, -1), lq2.reshape(1, -1),
      lk2.reshape(1, -1), subln_g.reshape(1, -1))


def _mix_kernel(gate_ref, yc_ref, on_ref, wo_ref, wout_ref, x_ref, g_ref, b_ref, x1_ref, x1b_ref):
    y_attn = _dot(on_ref[...], wo_ref[...])
    mixed = gate_ref[:, :D_MODEL] * yc_ref[...] + gate_ref[:, D_MODEL:] * y_attn
    z = _dot(mixed.astype(jnp.bfloat16), wout_ref[...])
    x1 = _layer_norm(ALPHA * x_ref[...] + z, g_ref[...], b_ref[...])
    x1_ref[...] = x1
    x1b_ref[...] = x1.astype(jnp.bfloat16)


def _mix(gates, y_conv, o_norm, w_o_b, w_out_b, x2d, ln_g, ln_b):
    t = x2d.shape[0]
    tm = 256
    row = lambda n: pl.BlockSpec((tm, n), lambda i: (i, 0))
    full = lambda a, b: pl.BlockSpec((a, b), lambda i: (0, 0))
    return pl.pallas_call(
        _mix_kernel, name="mix_deepnorm",
        grid=(t // tm,),
        in_specs=[row(COL_GATE), row(D_MODEL), row(ATTN_HEADS * ATTN_V_DIM),
                  full(ATTN_HEADS * ATTN_V_DIM, D_MODEL), full(D_MODEL, D_MODEL), row(D_MODEL),
                  full(1, D_MODEL), full(1, D_MODEL)],
        out_specs=[row(D_MODEL), row(D_MODEL)],
        out_shape=[jax.ShapeDtypeStruct((t, D_MODEL), jnp.float32),
                   jax.ShapeDtypeStruct((t, D_MODEL), jnp.bfloat16)],
        compiler_params=_params("parallel"),
    )(gates, y_conv, o_norm, w_o_b, w_out_b, x2d, ln_g.reshape(1, -1), ln_b.reshape(1, -1))


def _peer_prep_kernel(x_ref, wq_ref, keys_ref, cnt_ref, e1_ref, rank2_ref, e2_ref, top_ref, s1_ref, *, tm):
    q = _dot(x_ref[...], wq_ref[...]).astype(jnp.bfloat16)
    for c in range(2):
        for h in range(PEER_HEADS):
            idx = h * 2 + c
            s = _dot_nt(keys_ref[idx], q[:, idx * PEER_HALF:(idx + 1) * PEER_HALF])
            work = s
            rank = jnp.full(s.shape, float(PEER_TOPK), jnp.float32)
            for r in range(PEER_TOPK):
                mx = jnp.max(work, axis=0, keepdims=True)
                top_ref[c, r, pl.ds(h, 1), :] = mx
                hit = work == mx
                if c == 1:
                    rank = jnp.where(hit, float(r), rank)
                work = jnp.where(hit, NEG_INF, work)
            if c == 0:
                s1_ref[h] = s
            else:
                rank2_ref[h] = rank.astype(rank2_ref.dtype)
                e2_ref[h] = jnp.exp(s - top_ref[1, 0, pl.ds(h, 1), :]).astype(e2_ref.dtype)
    cands = [top_ref[0, i] + top_ref[1, j] for (i, j) in CAND_PAIRS]
    work = list(cands)
    remaining = jnp.full((PEER_HEADS, tm), float(PEER_TOPK), jnp.float32)
    tau = jnp.full((PEER_HEADS, tm), NEG_INF, jnp.float32)
    for _ in range(PEER_TOPK):
        mx = functools.reduce(jnp.maximum, work)
        hit = [w == mx for w in work]
        cnt = functools.reduce(jnp.add, [hm.astype(jnp.float32) for hm in hit])
        newly = jnp.logical_and(remaining > 0.0, cnt >= remaining)
        tau = jnp.where(newly, mx, tau)
        remaining = remaining - cnt
        work = [jnp.where(hm, NEG_INF, w) for hm, w in zip(hit, work)]
    m1 = top_ref[0, 0]
    m2 = top_ref[1, 0]
    ex1 = [jnp.exp(top_ref[0, i] - m1) for i in range(PEER_TOPK)]
    ex2 = [jnp.exp(top_ref[1, j] - m2) for j in range(PEER_TOPK)]
    z = jnp.zeros((PEER_HEADS, tm), jnp.float32)
    sel_count = [jnp.zeros((PEER_HEADS, tm), jnp.float32) for _ in range(PEER_TOPK)]
    for cand, (i, j) in zip(cands, CAND_PAIRS):
        sel = cand >= tau
        z = z + jnp.where(sel, ex1[i] * ex2[j], 0.0)
        sel_count[i] = sel_count[i] + sel.astype(jnp.float32)
    for h in range(PEER_HEADS):
        s1 = s1_ref[h]
        cnt = jnp.zeros(s1.shape, jnp.float32)
        for i in range(PEER_TOPK):
            cnt = jnp.where(s1 == top_ref[0, i, pl.ds(h, 1), :], sel_count[i][h:h + 1, :], cnt)
        cnt_ref[h] = cnt
        e1_ref[h] = jnp.exp(s1 - m1[h:h + 1, :]) / z[h:h + 1, :]


def _peer_prep(x1b, wq_b, keys_b):
    t = x1b.shape[0]
    tm = 256
    nq = 2 * PEER_HEADS * PEER_HALF
    big = lambda: pl.BlockSpec((PEER_HEADS, PEER_N_KEYS, tm), lambda i: (0, 0, i))
    f32_shape = jax.ShapeDtypeStruct((PEER_HEADS, PEER_N_KEYS, t), jnp.float32)
    bf16_shape = jax.ShapeDtypeStruct((PEER_HEADS, PEER_N_KEYS, t), jnp.bfloat16)
    return pl.pallas_call(
        functools.partial(_peer_prep_kernel, tm=tm), name="peer_prep",
        grid=(t // tm,),
        in_specs=[pl.BlockSpec((tm, D_MODEL), lambda i: (i, 0)),
                  pl.BlockSpec((D_MODEL, nq), lambda i: (0, 0)),
                  pl.BlockSpec((2 * PEER_HEADS, PEER_N_KEYS, PEER_HALF), lambda i: (0, 0, 0))],
        out_specs=[big(), big(), big(), big()],
        out_shape=[f32_shape, f32_shape, bf16_shape, bf16_shape],
        scratch_shapes=[pltpu.VMEM((2, PEER_TOPK, PEER_HEADS, tm), jnp.float32),
                        pltpu.VMEM((PEER_HEADS, PEER_N_KEYS, tm), jnp.float32)],
        compiler_params=_params("parallel"),
    )(x1b, wq_b, keys_b)


def _gelu_tanh(x):
    c = math.sqrt(2.0 / math.pi)
    half = 0.5 * x
    return half + half * jnp.tanh(x * (c + (c * 0.044715) * (x * x)))


BF16_ROWS = 16


def _peer_dense_kernel(xb_ref, u_ref, v_ref, cnt_ref, e1_ref, rank2_ref, e2_ref, o_ref, acc_ref, wd_ref, *, tm, te):
    j = pl.program_id(1)
    n_tiles = pl.num_programs(1) - 1
    jc = jnp.minimum(j, n_tiles - 1)

    @pl.when(j == 0)
    def _():
        acc_ref[...] = jnp.zeros_like(acc_ref)
        wd_ref[...] = jnp.zeros_like(wd_ref)

    acc_ref[...] += _dot_tn(wd_ref[...], v_ref[...])
    hid = _dot_nt(u_ref[...], xb_ref[...])
    bf = jnp.bfloat16
    groups = PEER_N_KEYS // BF16_ROWS
    for k in range(te // PEER_N_KEYS):
        a = jc * (te // PEER_N_KEYS) + k
        gate = [jnp.zeros((BF16_ROWS, tm), bf) for _ in range(groups)]
        for h in range(PEER_HEADS):
            cnt = jnp.broadcast_to(cnt_ref[h, pl.ds(a, 1), :], (BF16_ROWS, tm)).astype(bf)
            e1 = jnp.broadcast_to(e1_ref[h, pl.ds(a, 1), :], (BF16_ROWS, tm)).astype(bf)
            for g in range(groups):
                rows = pl.ds(g * BF16_ROWS, BF16_ROWS)
                sel = rank2_ref[h, rows, :] < cnt
                gate[g] = gate[g] + jnp.where(sel, e1 * e2_ref[h, rows, :], jnp.zeros((), bf))
        for g in range(groups):
            r0 = k * PEER_N_KEYS + g * BF16_ROWS
            wd_ref[pl.ds(r0, BF16_ROWS), :] = gate[g] * _gelu_tanh(hid[r0:r0 + BF16_ROWS, :]).astype(bf)

    @pl.when(j == n_tiles)
    def _():
        o_ref[...] = acc_ref[...]


def _peer_dense(x1b, u_b, v_b, cnt, e1, rank2, e2):
    t = x1b.shape[0]
    n_exp = u_b.shape[0]
    tm, te = 512, 512
    n_tiles = n_exp // te
    big = lambda: pl.BlockSpec((PEER_HEADS, PEER_N_KEYS, tm), lambda i, j: (0, 0, i))
    row = lambda: pl.BlockSpec((tm, D_MODEL), lambda i, j: (i, 0))
    return pl.pallas_call(
        functools.partial(_peer_dense_kernel, tm=tm, te=te), name="peer_dense",
        grid=(t // tm, n_tiles + 1),
        in_specs=[row(),
                  pl.BlockSpec((te, D_MODEL), lambda i, j: (jnp.minimum(j, n_tiles - 1), 0)),
                  pl.BlockSpec((te, D_MODEL), lambda i, j: (jnp.maximum(j - 1, 0), 0)),
                  big(), big(), big(), big()],
        out_specs=row(),
        out_shape=jax.ShapeDtypeStruct((t, D_MODEL), jnp.float32),
        scratch_shapes=[pltpu.VMEM((tm, D_MODEL), jnp.float32), pltpu.VMEM((te, tm), jnp.bfloat16)],
        compiler_params=_params("parallel", "arbitrary"),
    )(x1b, u_b, v_b, cnt, e1, rank2, e2)


def _ple_kernel(x1_ref, po_ref, g_ref, b_ref, wg_ref, bg_ref, p_ref, wp_ref, o_ref):
    x2 = _layer_norm(ALPHA * x1_ref[...] + po_ref[...], g_ref[...], b_ref[...])
    gate = jax.nn.sigmoid(_dot(x2.astype(jnp.bfloat16), wg_ref[...]) + bg_ref[...])
    o_ref[...] = x2 + gate * _dot(p_ref[...], wp_ref[...])


def _ple(x1, peer_out, ln_g, ln_b, w_gate_b, b_gate, p_b, w_proj_b):
    t = x1.shape[0]
    tm = 512
    row = lambda: pl.BlockSpec((tm, D_MODEL), lambda i: (i, 0))
    vec = lambda: pl.BlockSpec((1, D_MODEL), lambda i: (0, 0))
    return pl.pallas_call(
        _ple_kernel, name="ple",
        grid=(t // tm,),
        in_specs=[row(), row(), vec(), vec(),
                  pl.BlockSpec((D_MODEL, D_MODEL), lambda i: (0, 0)),
                  vec(),
                  pl.BlockSpec((tm, P_DIM), lambda i: (i, 0)),
                  pl.BlockSpec((P_DIM, D_MODEL), lambda i: (0, 0))],
        out_specs=row(),
        out_shape=jax.ShapeDtypeStruct((t, D_MODEL), jnp.float32),
        compiler_params=_params("parallel"),
    )(x1, peer_out, ln_g.reshape(1, -1), ln_b.reshape(1, -1), w_gate_b, b_gate.reshape(1, -1), p_b, w_proj_b)


def kernel(x, p, w_in, b_gate, conv_dw_w, conv_dw_b, conv_ln_g, conv_ln_b, conv_w_out, attn_lambda_q1, attn_lambda_k1, attn_lambda_q2, attn_lambda_k2, attn_subln_g, attn_w_o, w_out, ln1_g, ln1_b, peer_w_q, peer_sub_keys, peer_u, peer_v, ln2_g, ln2_b, ple_w_proj, ple_w_gate, ple_b_gate):
    batch, seq, d = x.shape
    t = batch * seq
    bf = jnp.bfloat16
    assert w_in.shape[0] == DEPTH == 1 and d == D_MODEL
    x2d = x.reshape(t, d)

    glu, qkv, gates = _input_projection(x2d.astype(bf), w_in[0].astype(bf), b_gate[0])
    y_conv = _conformer_conv(glu, conv_dw_w[0], conv_dw_b[0], conv_ln_g[0], conv_ln_b[0],
                             conv_w_out[0].astype(bf), batch, seq)
    o_norm = _diff_attention(qkv, attn_lambda_q1[0], attn_lambda_k1[0], attn_lambda_q2[0],
                             attn_lambda_k2[0], attn_subln_g[0], batch, seq)
    x1, x1b = _mix(gates, y_conv, o_norm.reshape(t, -1), attn_w_o[0].astype(bf), w_out[0].astype(bf),
                   x2d, ln1_g[0], ln1_b[0])

    wq = peer_w_q[0].astype(bf)
    keys = peer_sub_keys[0].reshape(2 * PEER_HEADS, PEER_N_KEYS, PEER_HALF).astype(bf)
    cnt, e1, rank2, e2 = _peer_prep(x1b, wq, keys)
    peer_out = _peer_dense(x1b, peer_u[0].astype(bf), peer_v[0].astype(bf), cnt, e1, rank2, e2)

    out = _ple(x1, peer_out, ln2_g[0], ln2_b[0], ple_w_gate[0].astype(bf), ple_b_gate[0],
               p[0].reshape(t, P_DIM).astype(bf), ple_w_proj[0].astype(bf))
    return out.reshape(batch, seq, d)
```

```python
import functools
import math

import jax
import jax.numpy as jnp
import numpy as np
from jax import lax
from jax.experimental import pallas as pl
from jax.experimental.pallas import tpu as pltpu

D_MODEL = 2048
CONV_CH = 1024
CONV_WIDTH = 31
ATTN_HEADS = 8
ATTN_HEAD_DIM = 64
ATTN_V_DIM = 2 * ATTN_HEAD_DIM
PEER_HEADS = 8
PEER_N_KEYS = 128
PEER_HALF = 128
PEER_TOPK = 16
P_DIM = 256
LN_EPS = 1e-5
DEPTH = 1
ALPHA = (2.0 * DEPTH) ** 0.25
LAM_INIT = 0.8 - 0.6 * math.exp(-0.3 * 0)

COL_CONV = 2 * CONV_CH
COL_QKV = 3 * ATTN_HEADS * ATTN_V_DIM
COL_GATE = 2 * D_MODEL

VMEM_LIMIT_BYTES = 56 * 1024 * 1024
NEG_INF = float("-inf")

CAND_PAIRS = tuple((i, j) for i in range(PEER_TOPK) for j in range(PEER_TOPK)
                   if (i + 1) * (j + 1) <= PEER_TOPK)


def _params(*sem, **kw):
    return pltpu.CompilerParams(dimension_semantics=sem, vmem_limit_bytes=VMEM_LIMIT_BYTES, **kw)


def _dot(a, b):
    return jnp.dot(a, b, preferred_element_type=jnp.float32)


def _dot_nt(a, b):
    return lax.dot_general(a, b, (((1,), (1,)), ((), ())), preferred_element_type=jnp.float32)


def _dot_tn(a, b):
    return lax.dot_general(a, b, (((0,), (0,)), ((), ())), preferred_element_type=jnp.float32)


def _layer_norm(r, g, b):
    mu = jnp.mean(r, axis=-1, keepdims=True)
    c = r - mu
    var = jnp.mean(c * c, axis=-1, keepdims=True)
    return c * lax.rsqrt(var + LN_EPS) * g + b


def _proj_glu_kernel(x_ref, wa_ref, wb_ref, o_ref):
    x = x_ref[...]
    o_ref[...] = _dot(x, wa_ref[...]) * jax.nn.sigmoid(_dot(x, wb_ref[...]))


def _proj_qkv_kernel(x_ref, w_ref, o_ref):
    o_ref[...] = _dot(x_ref[...], w_ref[...]).astype(o_ref.dtype)


def _proj_gate_kernel(x_ref, w_ref, b_ref, o_ref):
    o_ref[...] = jax.nn.sigmoid(_dot(x_ref[...], w_ref[...]) + b_ref[...])


def _input_projection(xb, w_in_b, b_gate):
    t = xb.shape[0]
    tm = 1024
    x_spec = pl.BlockSpec((tm, D_MODEL), lambda i, j: (i, 0))

    def w_spec(col0, tn):
        return pl.BlockSpec((D_MODEL, tn), lambda i, j: (0, j + col0 // tn))

    def o_spec(tn):
        return pl.BlockSpec((tm, tn), lambda i, j: (i, j))

    tn = 512
    glu = pl.pallas_call(
        _proj_glu_kernel, name="proj_glu",
        grid=(t // tm, CONV_CH // tn),
        in_specs=[x_spec, w_spec(0, tn), w_spec(CONV_CH, tn)],
        out_specs=o_spec(tn),
        out_shape=jax.ShapeDtypeStruct((t, CONV_CH), jnp.float32),
        compiler_params=_params("parallel", "parallel"),
    )(xb, w_in_b, w_in_b)
    tn = 1024
    qkv = pl.pallas_call(
        _proj_qkv_kernel, name="proj_qkv",
        grid=(t // tm, COL_QKV // tn),
        in_specs=[x_spec, w_spec(COL_CONV, tn)],
        out_specs=o_spec(tn),
        out_shape=jax.ShapeDtypeStruct((t, COL_QKV), jnp.bfloat16),
        compiler_params=_params("parallel", "parallel"),
    )(xb, w_in_b)
    gates = pl.pallas_call(
        _proj_gate_kernel, name="proj_gate",
        grid=(t // tm, COL_GATE // tn),
        in_specs=[x_spec, w_spec(COL_CONV + COL_QKV, tn), pl.BlockSpec((1, tn), lambda i, j: (0, j))],
        out_specs=o_spec(tn),
        out_shape=jax.ShapeDtypeStruct((t, COL_GATE), jnp.float32),
        compiler_params=_params("parallel", "parallel"),
    )(xb, w_in_b, b_gate.reshape(1, COL_GATE))
    return glu, qkv, gates


SUBLANES = 8
CONV_HALO = 32


def _conv_kernel(prev_ref, cur_ref, dww_ref, dwb_ref, lng_ref, lnb_ref, wpw_ref, o_ref, ext_ref, *, ts):
    i = pl.program_id(1)
    halo = prev_ref[0]
    ext_ref[pl.ds(0, CONV_HALO), :] = jnp.where(i > 0, halo, jnp.zeros_like(halo))
    ext_ref[pl.ds(CONV_HALO, ts), :] = cur_ref[0]
    off0 = CONV_HALO - (CONV_WIDTH - 1)
    acc = jnp.zeros((ts, CONV_CH), jnp.float32)
    ext = ext_ref[...]
    n_ext = CONV_HALO + ts
    for r in range(SUBLANES):
        offsets = [o for o in range(r, CONV_HALO + 1, SUBLANES) if 0 <= o - off0 < CONV_WIDTH]
        shifted = ext if r == 0 else pltpu.roll(ext, n_ext - r, axis=0)
        for o in offsets:
            acc = acc + shifted[o - r:o - r + ts, :] * dww_ref[pl.ds(o - off0, 1), :]
    acc = acc + dwb_ref[...]
    y = _layer_norm(acc, lng_ref[...], lnb_ref[...])
    y = y * jax.nn.sigmoid(y)
    o_ref[...] = _dot(y.astype(jnp.bfloat16), wpw_ref[...])


def _conformer_conv(glu, dw_w, dw_b, ln_g, ln_b, w_pw_b, batch, seq):
    ts = 256
    h3 = glu.reshape(batch, seq, CONV_CH)
    nblk = ts // CONV_HALO
    vec = lambda: pl.BlockSpec((1, CONV_CH), lambda b, i: (0, 0))
    return pl.pallas_call(
        functools.partial(_conv_kernel, ts=ts), name="conv_module",
        grid=(batch, seq // ts),
        in_specs=[
            pl.BlockSpec((1, CONV_HALO, CONV_CH), lambda b, i: (b, jnp.maximum(i * nblk - 1, 0), 0)),
            pl.BlockSpec((1, ts, CONV_CH), lambda b, i: (b, i, 0)),
            pl.BlockSpec((CONV_WIDTH, CONV_CH), lambda b, i: (0, 0)),
            vec(), vec(), vec(),
            pl.BlockSpec((CONV_CH, D_MODEL), lambda b, i: (0, 0)),
        ],
        out_specs=pl.BlockSpec((ts, D_MODEL), lambda b, i: (b * (seq // ts) + i, 0)),
        out_shape=jax.ShapeDtypeStruct((batch * seq, D_MODEL), jnp.float32),
        scratch_shapes=[pltpu.VMEM((CONV_HALO + ts, CONV_CH), jnp.float32)],
        compiler_params=_params("parallel", "parallel"),
    )(h3, h3, dw_w.reshape(CONV_WIDTH, CONV_CH), dw_b.reshape(1, CONV_CH),
      ln_g.reshape(1, CONV_CH), ln_b.reshape(1, CONV_CH), w_pw_b)


def _attn_kernel(slopes_ref, q_ref, k_ref, v_ref, lq1_ref, lk1_ref, lq2_ref, lk2_ref, g_ref, o_ref, *, tq):
    h = pl.program_id(1)
    qi = pl.program_id(2)
    slope = slopes_ref[h]
    q = q_ref[0] * (ATTN_HEAD_DIM ** -0.5)
    lane = lax.broadcasted_iota(jnp.int32, q.shape, 1)
    zero = jnp.zeros_like(q)
    q1 = jnp.where(lane < ATTN_HEAD_DIM, q, zero)
    q2 = jnp.where(lane >= ATTN_HEAD_DIM, q, zero)
    causal = (lax.broadcasted_iota(jnp.int32, (tq, tq), 0)
              >= lax.broadcasted_iota(jnp.int32, (tq, tq), 1))
    kcol = lax.broadcasted_iota(jnp.int32, (1, tq), 1)

    def block(kb, carry, diagonal):
        m1, l1, a1, m2, l2, a2 = carry
        k = k_ref[0, pl.ds(kb * tq, tq), :]
        v = v_ref[0, pl.ds(kb * tq, tq), :]
        bias = slope * ((kb - qi) * tq + kcol).astype(jnp.float32)

        def one(qm, m, l, a):
            s = _dot_nt(qm, k) + bias
            if diagonal:
                s = jnp.where(causal, s, NEG_INF)
            m_new = jnp.maximum(m, jnp.max(s, axis=-1, keepdims=True))
            p = jnp.exp(s - m_new)
            corr = jnp.exp(m - m_new)
            l_new = corr * l + jnp.sum(p, axis=-1, keepdims=True)
            a_new = corr * a + _dot(p.astype(v.dtype), v)
            return m_new, l_new, a_new

        m1, l1, a1 = one(q1, m1, l1, a1)
        m2, l2, a2 = one(q2, m2, l2, a2)
        return m1, l1, a1, m2, l2, a2

    col = lambda val: jnp.full((tq, 1), val, jnp.float32)
    acc0 = jnp.zeros((tq, ATTN_V_DIM), jnp.float32)
    init = (col(NEG_INF), col(0.0), acc0, col(NEG_INF), col(0.0), acc0)
    carry = lax.fori_loop(0, qi, functools.partial(block, diagonal=False), init)
    m1, l1, a1, m2, l2, a2 = block(qi, carry, diagonal=True)

    lam = (jnp.exp(jnp.sum(lq1_ref[...] * lk1_ref[...], axis=-1, keepdims=True))
           - jnp.exp(jnp.sum(lq2_ref[...] * lk2_ref[...], axis=-1, keepdims=True)) + LAM_INIT)
    o = a1 / l1 - lam * (a2 / l2)
    ms = jnp.mean(o * o, axis=-1, keepdims=True)
    o = o * lax.rsqrt(ms + LN_EPS) * g_ref[...] * (1.0 - LAM_INIT)
    o_ref[0] = o.astype(o_ref.dtype)


def _diff_attention(qkv, lq1, lk1, lq2, lk2, subln_g, batch, seq):
    tq = 2048
    qkv3 = qkv.reshape(batch, seq, COL_QKV)
    slopes = jnp.asarray(2.0 ** (-8.0 * np.arange(1, ATTN_HEADS + 1) / ATTN_HEADS), jnp.float32)
    hd = lambda: pl.BlockSpec((1, ATTN_HEAD_DIM), lambda b, h, i: (0, 0))
    return pl.pallas_call(
        functools.partial(_attn_kernel, tq=tq), name="diff_attention",
        grid=(batch, ATTN_HEADS, seq // tq),
        in_specs=[
            pl.BlockSpec(memory_space=pltpu.SMEM),
            pl.BlockSpec((1, tq, ATTN_V_DIM), lambda b, h, i: (b, i, h)),
            pl.BlockSpec((1, seq, ATTN_V_DIM), lambda b, h, i: (b, 0, ATTN_HEADS + h)),
            pl.BlockSpec((1, seq, ATTN_V_DIM), lambda b, h, i: (b, 0, 2 * ATTN_HEADS + h)),
            hd(), hd(), hd(), hd(),
            pl.BlockSpec((1, ATTN_V_DIM), lambda b, h, i: (0, 0)),
        ],
        out_specs=pl.BlockSpec((1, tq, ATTN_V_DIM), lambda b, h, i: (b, i, h)),
        out_shape=jax.ShapeDtypeStruct((batch, seq, ATTN_HEADS * ATTN_V_DIM), jnp.bfloat16),
        compiler_params=_params("parallel", "parallel", "arbitrary"),
    )(slopes, qkv3, qkv3, qkv3, lq1.reshape(1, -1), lk1.reshape(1, -1), lq2.reshape(1, -1),
      lk2.reshape(1, -1), subln_g.reshape(1, -1))


def _mix_kernel(gate_ref, yc_ref, on_ref, wo_ref, wout_ref, x_ref, g_ref, b_ref, x1_ref, x1b_ref):
    y_attn = _dot(on_ref[...], wo_ref[...])
    mixed = gate_ref[:, :D_MODEL] * yc_ref[...] + gate_ref[:, D_MODEL:] * y_attn
    z = _dot(mixed.astype(jnp.bfloat16), wout_ref[...])
    x1 = _layer_norm(ALPHA * x_ref[...] + z, g_ref[...], b_ref[...])
    x1_ref[...] = x1
    x1b_ref[...] = x1.astype(jnp.bfloat16)


def _mix(gates, y_conv, o_norm, w_o_b, w_out_b, x2d, ln_g, ln_b):
    t = x2d.shape[0]
    tm = 256
    row = lambda n: pl.BlockSpec((tm, n), lambda i: (i, 0))
    full = lambda a, b: pl.BlockSpec((a, b), lambda i: (0, 0))
    return pl.pallas_call(
        _mix_kernel, name="mix_deepnorm",
        grid=(t // tm,),
        in_specs=[row(COL_GATE), row(D_MODEL), row(ATTN_HEADS * ATTN_V_DIM),
                  full(ATTN_HEADS * ATTN_V_DIM, D_MODEL), full(D_MODEL, D_MODEL), row(D_MODEL),
                  full(1, D_MODEL), full(1, D_MODEL)],
        out_specs=[row(D_MODEL), row(D_MODEL)],
        out_shape=[jax.ShapeDtypeStruct((t, D_MODEL), jnp.float32),
                   jax.ShapeDtypeStruct((t, D_MODEL), jnp.bfloat16)],
        compiler_params=_params("parallel"),
    )(gates, y_conv, o_norm, w_o_b, w_out_b, x2d, ln_g.reshape(1, -1), ln_b.reshape(1, -1))


def _peer_prep_kernel(x_ref, wq_ref, keys_ref, cnt_ref, e1_ref, rank2_ref, e2_ref, top_ref, s1_ref, *, tm):
    q = _dot(x_ref[...], wq_ref[...]).astype(jnp.bfloat16)
    for c in range(2):
        for h in range(PEER_HEADS):
            idx = h * 2 + c
            s = _dot_nt(keys_ref[idx], q[:, idx * PEER_HALF:(idx + 1) * PEER_HALF])
            work = s
            rank = jnp.full(s.shape, float(PEER_TOPK), jnp.float32)
            for r in range(PEER_TOPK):
                mx = jnp.max(work, axis=0, keepdims=True)
                top_ref[c, r, pl.ds(h, 1), :] = mx
                hit = work == mx
                if c == 1:
                    rank = jnp.where(hit, float(r), rank)
                work = jnp.where(hit, NEG_INF, work)
            if c == 0:
                s1_ref[h] = s
            else:
                rank2_ref[h] = rank.astype(rank2_ref.dtype)
                e2_ref[h] = jnp.exp(s - top_ref[1, 0, pl.ds(h, 1), :]).astype(e2_ref.dtype)
    cands = [top_ref[0, i] + top_ref[1, j] for (i, j) in CAND_PAIRS]
    work = list(cands)
    remaining = jnp.full((PEER_HEADS, tm), float(PEER_TOPK), jnp.float32)
    tau = jnp.full((PEER_HEADS, tm), NEG_INF, jnp.float32)
    for _ in range(PEER_TOPK):
        mx = functools.reduce(jnp.maximum, work)
        hit = [w == mx for w in work]
        cnt = functools.reduce(jnp.add, [hm.astype(jnp.float32) for hm in hit])
        newly = jnp.logical_and(remaining > 0.0, cnt >= remaining)
        tau = jnp.where(newly, mx, tau)
        remaining = remaining - cnt
        work = [jnp.where(hm, NEG_INF, w) for hm, w in zip(hit, work)]
    m1 = top_ref[0, 0]
    m2 = top_ref[1, 0]
    ex1 = [jnp.exp(top_ref[0, i] - m1) for i in range(PEER_TOPK)]
    ex2 = [jnp.exp(top_ref[1, j] - m2) for j in range(PEER_TOPK)]
    z = jnp.zeros((PEER_HEADS, tm), jnp.float32)
    sel_count = [jnp.zeros((PEER_HEADS, tm), jnp.float32) for _ in range(PEER_TOPK)]
    for cand, (i, j) in zip(cands, CAND_PAIRS):
        sel = cand >= tau
        z = z + jnp.where(sel, ex1[i] * ex2[j], 0.0)
        sel_count[i] = sel_count[i] + sel.astype(jnp.float32)
    for h in range(PEER_HEADS):
        s1 = s1_ref[h]
        cnt = jnp.zeros(s1.shape, jnp.float32)
        for i in range(PEER_TOPK):
            cnt = jnp.where(s1 == top_ref[0, i, pl.ds(h, 1), :], sel_count[i][h:h + 1, :], cnt)
        cnt_ref[h] = cnt
        e1_ref[h] = jnp.exp(s1 - m1[h:h + 1, :]) / z[h:h + 1, :]


def _peer_prep(x1b, wq_b, keys_b):
    t = x1b.shape[0]
    tm = 256
    nq = 2 * PEER_HEADS * PEER_HALF
    big = lambda: pl.BlockSpec((PEER_HEADS, PEER_N_KEYS, tm), lambda i: (0, 0, i))
    f32_shape = jax.ShapeDtypeStruct((PEER_HEADS, PEER_N_KEYS, t), jnp.float32)
    bf16_shape = jax.ShapeDtypeStruct((PEER_HEADS, PEER_N_KEYS, t), jnp.bfloat16)
    return pl.pallas_call(
        functools.partial(_peer_prep_kernel, tm=tm), name="peer_prep",
        grid=(t // tm,),
        in_specs=[pl.BlockSpec((tm, D_MODEL), lambda i: (i, 0)),
                  pl.BlockSpec((D_MODEL, nq), lambda i: (0, 0)),
                  pl.BlockSpec((2 * PEER_HEADS, PEER_N_KEYS, PEER_HALF), lambda i: (0, 0, 0))],
        out_specs=[big(), big(), big(), big()],
        out_shape=[f32_shape, f32_shape, bf16_shape, bf16_shape],
        scratch_shapes=[pltpu.VMEM((2, PEER_TOPK, PEER_HEADS, tm), jnp.float32),
                        pltpu.VMEM((PEER_HEADS, PEER_N_KEYS, tm), jnp.float32)],
        compiler_params=_params("parallel"),
    )(x1b, wq_b, keys_b)


def _gelu_tanh(x):
    c = math.sqrt(2.0 / math.pi)
    half = 0.5 * x
    return half + half * jnp.tanh(x * (c + (c * 0.044715) * (x * x)))


BF16_ROWS = 16


def _peer_dense_kernel(xb_ref, u_ref, v_ref, cnt_ref, e1_ref, rank2_ref, e2_ref, o_ref, acc_ref, wd_ref, *, tm, te):
    j = pl.program_id(1)
    n_tiles = pl.num_programs(1) - 1
    jc = jnp.minimum(j, n_tiles - 1)

    @pl.when(j == 0)
    def _():
        acc_ref[...] = jnp.zeros_like(acc_ref)
        wd_ref[...] = jnp.zeros_like(wd_ref)

    acc_ref[...] += _dot_tn(wd_ref[...], v_ref[...])
    hid = _dot_nt(u_ref[...], xb_ref[...])
    bf = jnp.bfloat16
    groups = PEER_N_KEYS // BF16_ROWS
    for k in range(te // PEER_N_KEYS):
        a = jc * (te // PEER_N_KEYS) + k
        gate = [jnp.zeros((BF16_ROWS, tm), bf) for _ in range(groups)]
        for h in range(PEER_HEADS):
            cnt = jnp.broadcast_to(cnt_ref[h, pl.ds(a, 1), :], (BF16_ROWS, tm)).astype(bf)
            e1 = jnp.broadcast_to(e1_ref[h, pl.ds(a, 1), :], (BF16_ROWS, tm)).astype(bf)
            for g in range(groups):
                rows = pl.ds(g * BF16_ROWS, BF16_ROWS)
                sel = rank2_ref[h, rows, :] < cnt
                gate[g] = gate[g] + jnp.where(sel, e1 * e2_ref[h, rows, :], jnp.zeros((), bf))
        for g in range(groups):
            r0 = k * PEER_N_KEYS + g * BF16_ROWS
            wd_ref[pl.ds(r0, BF16_ROWS), :] = gate[g] * _gelu_tanh(hid[r0:r0 + BF16_ROWS, :]).astype(bf)

    @pl.when(j == n_tiles)
    def _():
        o_ref[...] = acc_ref[...]


def _peer_dense(x1b, u_b, v_b, cnt, e1, rank2, e2):
    t = x1b.shape[0]
    n_exp = u_b.shape[0]
    tm, te = 512, 512
    n_tiles = n_exp // te
    big = lambda: pl.BlockSpec((PEER_HEADS, PEER_N_KEYS, tm), lambda i, j: (0, 0, i))
    row = lambda: pl.BlockSpec((tm, D_MODEL), lambda i, j: (i, 0))
    return pl.pallas_call(
        functools.partial(_peer_dense_kernel, tm=tm, te=te), name="peer_dense",
        grid=(t // tm, n_tiles + 1),
        in_specs=[row(),
                  pl.BlockSpec((te, D_MODEL), lambda i, j: (jnp.minimum(j, n_tiles - 1), 0)),
                  pl.BlockSpec((te, D_MODEL), lambda i, j: (jnp.maximum(j - 1, 0), 0)),
                  big(), big(), big(), big()],
        out_specs=row(),
        out_shape=jax.ShapeDtypeStruct((t, D_MODEL), jnp.float32),
        scratch_shapes=[pltpu.VMEM((tm, D_MODEL), jnp.float32), pltpu.VMEM((te, tm), jnp.bfloat16)],
        compiler_params=_params("parallel", "arbitrary"),
    )(x1b, u_b, v_b, cnt, e1, rank2, e2)


def _ple_kernel(x1_ref, po_ref, g_ref, b_ref, wg_ref, bg_ref, p_ref, wp_ref, o_ref):
    x2 = _layer_norm(ALPHA * x1_ref[...] + po_ref[...], g_ref[...], b_ref[...])
    gate = jax.nn.sigmoid(_dot(x2.astype(jnp.bfloat16), wg_ref[...]) + bg_ref[...])
    o_ref[...] = x2 + gate * _dot(p_ref[...], wp_ref[...])


def _ple(x1, peer_out, ln_g, ln_b, w_gate_b, b_gate, p_b, w_proj_b):
    t = x1.shape[0]
    tm = 512
    row = lambda: pl.BlockSpec((tm, D_MODEL), lambda i: (i, 0))
    vec = lambda: pl.BlockSpec((1, D_MODEL), lambda i: (0, 0))
    return pl.pallas_call(
        _ple_kernel, name="ple",
        grid=(t // tm,),
        in_specs=[row(), row(), vec(), vec(),
                  pl.BlockSpec((D_MODEL, D_MODEL), lambda i: (0, 0)),
                  vec(),
                  pl.BlockSpec((tm, P_DIM), lambda i: (i, 0)),
                  pl.BlockSpec((P_DIM, D_MODEL), lambda i: (0, 0))],
        out_specs=row(),
        out_shape=jax.ShapeDtypeStruct((t, D_MODEL), jnp.float32),
        compiler_params=_params("parallel"),
    )(x1, peer_out, ln_g.reshape(1, -1), ln_b.reshape(1, -1), w_gate_b, b_gate.reshape(1, -1), p_b, w_proj_b)


def kernel(x, p, w_in, b_gate, conv_dw_w, conv_dw_b, conv_ln_g, conv_ln_b, conv_w_out, attn_lambda_q1, attn_lambda_k1, attn_lambda_q2, attn_lambda_k2, attn_subln_g, attn_w_o, w_out, ln1_g, ln1_b, peer_w_q, peer_sub_keys, peer_u, peer_v, ln2_g, ln2_b, ple_w_proj, ple_w_gate, ple_b_gate):
    batch, seq, d = x.shape
    t = batch * seq
    bf = jnp.bfloat16
    assert w_in.shape[0] == DEPTH == 1 and d == D_MODEL
    x2d = x.reshape(t, d)

    glu, qkv, gates = _input_projection(x2d.astype(bf), w_in[0].astype(bf), b_gate[0])
    y_conv = _conformer_conv(glu, conv_dw_w[0], conv_dw_b[0], conv_ln_g[0], conv_ln_b[0],
                             conv_w_out[0].astype(bf), batch, seq)
    o_norm = _diff_attention(qkv, attn_lambda_q1[0], attn_lambda_k1[0], attn_lambda_q2[0],
                             attn_lambda_k2[0], attn_subln_g[0], batch, seq)
    x1, x1b = _mix(gates, y_conv, o_norm.reshape(t, -1), attn_w_o[0].astype(bf), w_out[0].astype(bf),
                   x2d, ln1_g[0], ln1_b[0])

    wq = peer_w_q[0].astype(bf)
    keys = peer_sub_keys[0].reshape(2 * PEER_HEADS, PEER_N_KEYS, PEER_HALF).astype(bf)
    cnt, e1, rank2, e2 = _peer_prep(x1b, wq, keys)
    peer_out = _peer_dense(x1b, peer_u[0].astype(bf), peer_v[0].astype(bf), cnt, e1, rank2, e2)

    out = _ple(x1, peer_out, ln2_g[0], ln2_b[0], ple_w_gate[0].astype(bf), ple_b_gate[0],
               p[0].reshape(t, P_DIM).astype(bf), ple_w_proj[0].astype(bf))
    return out.reshape(batch, seq, d)
```
